```python
import jax, jax.numpy as jnp
from jax import lax
import numpy as np

D_MODEL = 2048
BATCH = 8
SEQ = 2048
DEPTH = 1
DEC_BATCH = 1
DEC_SEQ = 16384
PAST_LEN = 128

HEAD_DIM = 128
N_HEADS_NA = 8
N_HEADS_GQA = 8
N_KV_GQA = 2
GQA_GROUP = N_HEADS_GQA // N_KV_GQA
WIDTH_NA = N_HEADS_NA * HEAD_DIM
WIDTH_GQA = N_HEADS_GQA * HEAD_DIM
MIX_WIDTH = WIDTH_NA + WIDTH_GQA
KV_WIDTH_GQA = N_KV_GQA * HEAD_DIM
IN_COLS = 3 * WIDTH_NA + WIDTH_GQA + 2 * KV_WIDTH_GQA
GRID_W = 64
NA_KH_MAX = 8
NA_KW = 16
NA_KEY_SPAN = 2 * NA_KW
NA_N_CHUNKS = GRID_W // NA_KW
Q_BLOCK = 128
ROPE_THETA = 10000.0
AXIS_DIM = HEAD_DIM // 2
N_GROUPS = 4
EXPERTS_PER_GROUP = 8
N_EXPERTS = N_GROUPS * EXPERTS_PER_GROUP
TOP_K = 2
D_EXPERT = 512
MOE_BLOCK = 128
EPS = 1e-6
NEG = -1e30

kernel_name = "hymba_natten_gqa_hiermoe_encoder"


def rms_norm(x, g):
    xf = x.astype(jnp.float32)
    y = xf * lax.rsqrt(jnp.mean(xf * xf, axis=-1, keepdims=True) + EPS)
    return (y * g.astype(jnp.float32)).astype(x.dtype)


def neighbourhood_attention(q, k, v, rpb):
    B, L, H, Dh = q.shape
    rows = L // GRID_W
    kh = min(NA_KH_MAX, rows)
    cols = np.arange(GRID_W).reshape(NA_N_CHUNKS, NA_KW)
    win_start = np.clip(cols - NA_KW // 2, 0, GRID_W - NA_KW)
    chunk_start = np.clip(np.arange(NA_N_CHUNKS) * NA_KW - NA_KW // 2, 0, GRID_W - NA_KEY_SPAN)
    key_cols = chunk_start[:, None] + np.arange(NA_KEY_SPAN)
    kc3 = key_cols[:, None, :]
    col_mask = (kc3 >= win_start[:, :, None]) & (kc3 < win_start[:, :, None] + NA_KW)
    col_off = np.clip(kc3 - cols[:, :, None] + NA_KW - 1, 0, 2 * NA_KW - 2)
    col_bias = jnp.where(col_mask, rpb[:, :, col_off].astype(jnp.float32), NEG)

    qg = q.reshape(B, rows, NA_N_CHUNKS, NA_KW, H, Dh).transpose(1, 0, 2, 3, 4, 5)
    kg = k.reshape(B, rows, GRID_W, H, Dh)[:, :, key_cols]
    vg = v.reshape(B, rows, GRID_W, H, Dh)[:, :, key_cols]

    def row_step(args):
        r, q_row = args
        rs = jnp.clip(r - kh // 2, 0, rows - kh)
        k_band = lax.dynamic_slice_in_dim(kg, rs, kh, axis=1)
        v_band = lax.dynamic_slice_in_dim(vg, rs, kh, axis=1)
        s = jnp.einsum('bcqhd,bjcshd->bhcqjs', q_row, k_band,
                       preferred_element_type=jnp.float32)
        ro = rs + jnp.arange(kh) - r + NA_KH_MAX - 1
        rb = jnp.take(col_bias, ro, axis=1).transpose(0, 2, 3, 1, 4)
        p = jax.nn.softmax(s + rb, axis=(-2, -1))
        return jnp.einsum('bhcqjs,bjcshd->bcqhd', p.astype(v.dtype), v_band)

    o = lax.map(row_step, (jnp.arange(rows), qg))
    return o.transpose(1, 0, 2, 3, 4, 5).reshape(B, L, H, Dh)


def axial_rope(x):
    L = x.shape[1]
    t = jnp.arange(L)
    row = (t // GRID_W).astype(jnp.float32)
    col = (t % GRID_W).astype(jnp.float32)
    inv = ROPE_THETA ** (-jnp.arange(0, AXIS_DIM, 2, dtype=jnp.float32) / AXIS_DIM)
    xf = x.astype(jnp.float32)

    def rot(u, pos):
        ang = pos[:, None] * inv
        c = jnp.cos(ang)[None, :, None, :]
        s = jnp.sin(ang)[None, :, None, :]
        u1, u2 = jnp.split(u, 2, axis=-1)
        return jnp.concatenate([u1 * c - u2 * s, u2 * c + u1 * s], axis=-1)

    xr, xc = jnp.split(xf, 2, axis=-1)
    return jnp.concatenate([rot(xr, row), rot(xc, col)], axis=-1).astype(x.dtype)


def gqa_attention(q, k, v):
    B, L, HQ, Dh = q.shape
    nb = L // Q_BLOCK
    qb = q.reshape(B, nb, Q_BLOCK, N_KV_GQA, GQA_GROUP, Dh).transpose(1, 0, 2, 3, 4, 5)

    def block(qi):
        s = jnp.einsum('bqkgd,bskd->bkgqs', qi, k, preferred_element_type=jnp.float32)
        p = jax.nn.softmax(s, axis=-1)
        return jnp.einsum('bkgqs,bskd->bqkgd', p.astype(v.dtype), v)

    o = lax.map(block, qb)
    return o.transpose(1, 0, 2, 3, 4, 5).reshape(B, L, HQ, Dh)


def hier_router(h, w_rg, b_rg, w_re, b_re):
    lg = jnp.einsum('td,dg->tg', h, w_rg, preferred_element_type=jnp.float32) + b_rg.astype(jnp.float32)
    pg = jax.nn.softmax(lg, axis=-1)
    _, grp = lax.top_k(lg, 1)
    p_grp = jnp.take_along_axis(pg, grp, axis=-1)
    le = jnp.einsum('td,gde->tge', h, w_re, preferred_element_type=jnp.float32) + b_re.astype(jnp.float32)
    le_sel = jnp.take_along_axis(le, grp[:, :, None], axis=1)[:, 0]
    top_v, top_i = lax.top_k(le_sel, TOP_K)
    gate = p_grp * jax.nn.softmax(top_v, axis=-1)
    expert = grp * EXPERTS_PER_GROUP + top_i
    return expert, gate


def moe_ffn(h, expert, gate, w_gate, w_up, w_down):
    T, D = h.shape
    A = T * TOP_K
    e_flat = expert.reshape(A)
    tok_flat = jnp.arange(A) // TOP_K
    order = jnp.argsort(e_flat)
    e_sorted = e_flat[order]
    tok_sorted = tok_flat[order]
    counts = jnp.bincount(e_flat, length=N_EXPERTS)
    starts = jnp.cumsum(counts) - counts
    padded = (counts + MOE_BLOCK - 1) // MOE_BLOCK * MOE_BLOCK
    pad_end = jnp.cumsum(padded)
    pad_start = pad_end - padded
    dest = pad_start[e_sorted] + jnp.arange(A) - starts[e_sorted]
    n_blocks = -(-A // MOE_BLOCK) + N_EXPERTS
    row_tok = jnp.zeros((n_blocks * MOE_BLOCK,), jnp.int32).at[dest].set(tok_sorted.astype(jnp.int32))
    block_expert = jnp.minimum(
        jnp.searchsorted(pad_end, jnp.arange(n_blocks) * MOE_BLOCK, side='right'), N_EXPERTS - 1)
    xb = h[row_tok].reshape(n_blocks, MOE_BLOCK, D)

    def expert_block(args):
        xi, e = args
        return (jax.nn.silu(xi @ w_gate[e]) * (xi @ w_up[e])) @ w_down[e]

    yb = lax.map(expert_block, (xb, block_expert)).reshape(n_blocks * MOE_BLOCK, D)
    contrib = yb[dest] * gate.reshape(A)[order][:, None].astype(yb.dtype)
    return jnp.zeros_like(h).at[tok_sorted].add(contrib)


def encoder_layer(x, g_norm1, w_in, rpb, g_q, g_k, g_out_a, g_out_b, w_out,
                  g_norm2, w_rg, b_rg, w_re, b_re, w_gate, w_up, w_down):
    B, L, D = x.shape
    scale = HEAD_DIM ** -0.5
    h = rms_norm(x, g_norm1)
    proj = h @ w_in
    sizes = [WIDTH_NA, WIDTH_NA, WIDTH_NA, WIDTH_GQA, KV_WIDTH_GQA]
    cuts = np.cumsum(sizes).tolist()
    q_a, k_a, v_a, q_b, k_b, v_b = jnp.split(proj, cuts, axis=-1)
    o_a = neighbourhood_attention(q_a.reshape(B, L, N_HEADS_NA, HEAD_DIM) * scale,
                                  k_a.reshape(B, L, N_HEADS_NA, HEAD_DIM),
                                  v_a.reshape(B, L, N_HEADS_NA, HEAD_DIM), rpb)
    qh = axial_rope(rms_norm(q_b.reshape(B, L, N_HEADS_GQA, HEAD_DIM), g_q)) * scale
    kh = axial_rope(rms_norm(k_b.reshape(B, L, N_KV_GQA, HEAD_DIM), g_k))
    o_b = gqa_attention(qh, kh, v_b.reshape(B, L, N_KV_GQA, HEAD_DIM))
    mix = jnp.concatenate([rms_norm(o_a.reshape(B, L, WIDTH_NA), g_out_a),
                           rms_norm(o_b.reshape(B, L, WIDTH_GQA), g_out_b)], axis=-1)
    x = x + mix @ w_out
    h2 = rms_norm(x, g_norm2).reshape(B * L, D)
    expert, gate = hier_router(h2, w_rg, b_rg, w_re, b_re)
    x = x + moe_ffn(h2, expert, gate, w_gate, w_up, w_down).reshape(B, L, D)
    return x


def setup_inputs(seed: int = 0) -> dict:
    key = jax.random.key(seed)
    ks = jax.random.split(key, 20)
    f32 = jnp.float32
    nrm = lambda k, shape, s: jax.random.normal(k, shape, f32) * s
    gain = lambda k, shape: 1.0 + 0.05 * jax.random.normal(k, shape, f32)
    return {
        "x_prompt": jax.random.normal(ks[0], (BATCH, SEQ, D_MODEL), f32),
        "x_sample": jax.random.normal(ks[1], (DEC_BATCH, DEC_SEQ, D_MODEL), f32),
        "g_norm1": gain(ks[2], (DEPTH, D_MODEL)),
        "w_in": nrm(ks[3], (DEPTH, D_MODEL, IN_COLS), D_MODEL ** -0.5),
        "rpb": nrm(ks[4], (DEPTH, N_HEADS_NA, 2 * NA_KH_MAX - 1, 2 * NA_KW - 1), 0.1),
        "g_q": gain(ks[5], (DEPTH, HEAD_DIM)),
        "g_k": gain(ks[6], (DEPTH, HEAD_DIM)),
        "g_out_a": gain(ks[7], (DEPTH, WIDTH_NA)),
        "g_out_b": gain(ks[8], (DEPTH, WIDTH_GQA)),
        "w_out": nrm(ks[9], (DEPTH, MIX_WIDTH, D_MODEL), MIX_WIDTH ** -0.5),
        "g_norm2": gain(ks[10], (DEPTH, D_MODEL)),
        "w_router_group": nrm(ks[11], (DEPTH, D_MODEL, N_GROUPS), D_MODEL ** -0.5),
        "b_router_group": nrm(ks[12], (DEPTH, N_GROUPS), 0.01),
        "w_router_expert": nrm(ks[13], (DEPTH, N_GROUPS, D_MODEL, EXPERTS_PER_GROUP), D_MODEL ** -0.5),
        "b_router_expert": nrm(ks[14], (DEPTH, N_GROUPS, EXPERTS_PER_GROUP), 0.01),
        "w_gate": nrm(ks[15], (DEPTH, N_EXPERTS, D_MODEL, D_EXPERT), D_MODEL ** -0.5),
        "w_up": nrm(ks[16], (DEPTH, N_EXPERTS, D_MODEL, D_EXPERT), D_MODEL ** -0.5),
        "w_down": nrm(ks[17], (DEPTH, N_EXPERTS, D_EXPERT, D_MODEL), D_EXPERT ** -0.5),
        "g_norm_f": gain(ks[18], (D_MODEL,)),
    }


def reference(x_prompt, x_sample, g_norm1, w_in, rpb, g_q, g_k, g_out_a, g_out_b, w_out,
              g_norm2, w_router_group, b_router_group, w_router_expert, b_router_expert,
              w_gate, w_up, w_down, g_norm_f):
    def trunk(x):
        for layer in range(DEPTH):
            x = encoder_layer(x, g_norm1[layer], w_in[layer], rpb[layer], g_q[layer], g_k[layer],
                              g_out_a[layer], g_out_b[layer], w_out[layer], g_norm2[layer],
                              w_router_group[layer], b_router_group[layer],
                              w_router_expert[layer], b_router_expert[layer],
                              w_gate[layer], w_up[layer], w_down[layer])
        return rms_norm(x, g_norm_f)

    y_prompt = trunk(x_prompt)
    y_sample = trunk(x_sample)
    return (y_prompt, y_sample)
```

```python
import functools

import numpy as np
import jax
import jax.numpy as jnp
from jax import lax
from jax.experimental import pallas as pl
from jax.experimental.pallas import tpu as pltpu

F32 = jnp.float32
BF16 = jnp.bfloat16

D_MODEL = 2048
HEAD_DIM = 128
N_HEADS_NA = 8
N_HEADS_GQA = 8
N_KV_GQA = 2
GQA_GROUP = N_HEADS_GQA // N_KV_GQA
WIDTH_NA = N_HEADS_NA * HEAD_DIM
WIDTH_GQA = N_HEADS_GQA * HEAD_DIM
KV_WIDTH_GQA = N_KV_GQA * HEAD_DIM
IN_COLS = 3 * WIDTH_NA + WIDTH_GQA + 2 * KV_WIDTH_GQA
GRID_W = 64
NA_KH = 8
NA_KW = 16
ROPE_THETA = 10000.0
AXIS_DIM = HEAD_DIM // 2
N_GROUPS = 4
EXPERTS_PER_GROUP = 8
N_EXPERTS = N_GROUPS * EXPERTS_PER_GROUP
TOP_K = 2
D_EXPERT = 512
EPS = 1e-6
NEG = -1e30

COL_QA = 0
COL_KA = N_HEADS_NA
COL_VA = 2 * N_HEADS_NA
COL_QB = 3 * N_HEADS_NA
COL_KB = COL_QB + N_HEADS_GQA
COL_VB = COL_KB + N_KV_GQA

NA_QROWS = 4
NA_QB = NA_QROWS * GRID_W
NA_SEG = 3
PROJ_TN = 512
ROUTER_LANES = 128
ROUTER_HALF = 64
ROUTER_EXPERT_ROW0 = 8
VMEM_LIMIT = 56 * 1024 * 1024


def _tile(n, pref):
    t = min(n, pref)
    while n % t:
        t -= 1
    return t


def _rms(x, g):
    ms = jnp.mean(x * x, axis=-1, keepdims=True)
    return x * lax.rsqrt(ms + EPS) * g


def _in_proj_body(x_ref, g1_ref, w_ref, cos_ref, sa_ref, sb_ref, gq_ref, gk_ref, o_ref, h_scr):
    j = pl.program_id(1)

    @pl.when(j == 0)
    def _():
        h_scr[...] = _rms(x_ref[...], g1_ref[...]).astype(BF16)

    acc = jnp.dot(h_scr[...], w_ref[...], preferred_element_type=F32)
    scale = HEAD_DIM ** -0.5
    heads_per_tile = PROJ_TN // HEAD_DIM

    def rope(u, g, post):
        y = _rms(u, g)
        r = (y * cos_ref[...]
             + pltpu.roll(y, HEAD_DIM - AXIS_DIM // 2, axis=1) * sa_ref[...]
             + pltpu.roll(y, AXIS_DIM // 2, axis=1) * sb_ref[...])
        return r * post

    n_qa = WIDTH_NA // PROJ_TN
    n_plain_end = 3 * WIDTH_NA // PROJ_TN
    n_qb_end = n_plain_end + WIDTH_GQA // PROJ_TN

    @pl.when(j < n_qa)
    def _():
        o_ref[...] = (acc * scale).astype(o_ref.dtype)

    @pl.when((j >= n_qa) & (j < n_plain_end))
    def _():
        o_ref[...] = acc.astype(o_ref.dtype)

    @pl.when((j >= n_plain_end) & (j < n_qb_end))
    def _():
        for h in range(heads_per_tile):
            sl = slice(h * HEAD_DIM, (h + 1) * HEAD_DIM)
            o_ref[:, sl] = rope(acc[:, sl], gq_ref[...], scale).astype(o_ref.dtype)

    @pl.when(j >= n_qb_end)
    def _():
        for h in range(heads_per_tile):
            sl = slice(h * HEAD_DIM, (h + 1) * HEAD_DIM)
            if h < N_KV_GQA:
                o_ref[:, sl] = rope(acc[:, sl], gk_ref[...], 1.0).astype(o_ref.dtype)
            else:
                o_ref[:, sl] = acc[:, sl].astype(o_ref.dtype)


def _rope_tables(L):
    t = jnp.arange(L)
    row = (t // GRID_W).astype(F32)
    col = (t % GRID_W).astype(F32)
    inv = ROPE_THETA ** (-jnp.arange(0, AXIS_DIM, 2, dtype=F32) / AXIS_DIM)
    ang_r = row[:, None] * inv
    ang_c = col[:, None] * inv
    ang = jnp.concatenate([ang_r, ang_r, ang_c, ang_c], axis=-1)
    cos, sin = jnp.cos(ang), jnp.sin(ang)
    first_half = (np.arange(HEAD_DIM) % AXIS_DIM) < AXIS_DIM // 2
    sa = jnp.where(first_half, -sin, 0.0)
    sb = jnp.where(first_half, 0.0, sin)
    return cos, sa, sb


def _in_proj(x2, g1, w_bf, tables, gq, gk, L):
    T, D = x2.shape
    tm = _tile(L, 1024)
    n_pos = L // tm
    cos, sa, sb = tables
    assert IN_COLS % PROJ_TN == 0 and (IN_COLS - 2 * KV_WIDTH_GQA) % PROJ_TN == 0
    pos_spec = pl.BlockSpec((tm, HEAD_DIM), lambda i, j: (i % n_pos, 0))
    vec = lambda n: pl.BlockSpec((1, n), lambda i, j: (0, 0))
    return pl.pallas_call(
        _in_proj_body,
        grid=(T // tm, IN_COLS // PROJ_TN),
        in_specs=[
            pl.BlockSpec((tm, D), lambda i, j: (i, 0)),
            vec(D),
            pl.BlockSpec((D, PROJ_TN), lambda i, j: (0, j)),
            pos_spec, pos_spec, pos_spec,
            vec(HEAD_DIM), vec(HEAD_DIM),
        ],
        out_specs=pl.BlockSpec((tm, PROJ_TN), lambda i, j: (i, j)),
        out_shape=jax.ShapeDtypeStruct((T, IN_COLS), BF16),
        scratch_shapes=[pltpu.VMEM((tm, D), BF16)],
        compiler_params=pltpu.CompilerParams(
            dimension_semantics=("arbitrary", "arbitrary"), vmem_limit_bytes=VMEM_LIMIT),
        name="in_proj",
    )(x2, g1.reshape(1, D), w_bf, cos, sa, sb, gq.reshape(1, HEAD_DIM), gk.reshape(1, HEAD_DIM))


def _na_bias_tables(rpb):
    a = np.arange(NA_QROWS)[:, None, None, None, None]
    c = np.arange(GRID_W)[None, :, None, None, None]
    s = np.arange(NA_SEG)[None, None, :, None, None]
    j = np.arange(NA_QROWS)[None, None, None, :, None]
    kc = np.arange(GRID_W)[None, None, None, None, :]
    rel = NA_QROWS * (s - 1) + j
    dr = rel - a
    half = NA_KH // 2
    row_ok = np.stack([
        (rel >= 0) & (rel < NA_KH) & (a >= 0),
        (dr >= -half) & (dr < NA_KH - half),
        (rel >= NA_QROWS - NA_KH) & (rel < NA_QROWS) & (a >= 0),
    ])
    ws = np.clip(c - NA_KW // 2, 0, GRID_W - NA_KW)
    col_ok = (kc >= ws) & (kc < ws + NA_KW)
    ok = np.broadcast_to(row_ok & col_ok, (3, NA_QROWS, GRID_W, NA_SEG, NA_QROWS, GRID_W))
    ridx = np.broadcast_to(np.clip(dr + NA_KH - 1, 0, 2 * NA_KH - 2), ok.shape[1:])
    cidx = np.broadcast_to(np.clip(kc - c + NA_KW - 1, 0, 2 * NA_KW - 2), ok.shape[1:])
    ok = ok.reshape(3, NA_QB, NA_SEG * NA_QB)
    ridx = ridx.reshape(NA_QB, NA_SEG * NA_QB)
    cidx = cidx.reshape(NA_QB, NA_SEG * NA_QB)
    vals = rpb.astype(F32)[:, ridx, cidx]
    return jnp.where(ok[None], vals[:, None], NEG)


def _na_body(q_ref, k0_ref, k1_ref, k2_ref, v0_ref, v1_ref, v2_ref, b_ref, o_ref, *, nblk):
    i = pl.program_id(2)
    variant = jnp.where(i == 0, 0, jnp.where(i == nblk - 1, 2, 1))
    q = q_ref[...]
    dn = (((1,), (1,)), ((), ()))
    s = jnp.concatenate(
        [lax.dot_general(q, k[...], dn, preferred_element_type=F32) for k in (k0_ref, k1_ref, k2_ref)],
        axis=1)
    s = s + b_ref[0, variant]
    m = jnp.max(s, axis=-1, keepdims=True)
    p = jnp.exp(s - m)
    l = jnp.sum(p, axis=-1, keepdims=True)
    pb = p.astype(BF16)
    o = jnp.zeros((NA_QB, HEAD_DIM), F32)
    for seg, v in enumerate((v0_ref, v1_ref, v2_ref)):
        o = o + jnp.dot(pb[:, seg * NA_QB:(seg + 1) * NA_QB], v[...], preferred_element_type=F32)
    o_ref[...] = (o / l).astype(o_ref.dtype)


def _na_attention(proj, bias, B, L):
    T = proj.shape[0]
    rows = L // GRID_W
    assert L % GRID_W == 0 and rows % NA_QROWS == 0 and rows >= NA_KH
    nblk = rows // NA_QROWS

    def kv_spec(col0, shift):
        def imap(h, b, i):
            return (b * nblk + jnp.clip(i + shift, 0, nblk - 1), col0 + h)
        return pl.BlockSpec((NA_QB, HEAD_DIM), imap)

    return pl.pallas_call(
        functools.partial(_na_body, nblk=nblk),
        grid=(N_HEADS_NA, B, nblk),
        in_specs=[
            pl.BlockSpec((NA_QB, HEAD_DIM), lambda h, b, i: (b * nblk + i, COL_QA + h)),
            kv_spec(COL_KA, -1), kv_spec(COL_KA, 0), kv_spec(COL_KA, 1),
            kv_spec(COL_VA, -1), kv_spec(COL_VA, 0), kv_spec(COL_VA, 1),
            pl.BlockSpec((1, 3, NA_QB, NA_SEG * NA_QB), lambda h, b, i: (h, 0, 0, 0)),
        ],
        out_specs=pl.BlockSpec((NA_QB, HEAD_DIM), lambda h, b, i: (b * nblk + i, h)),
        out_shape=jax.ShapeDtypeStruct((T, WIDTH_NA), BF16),
        compiler_params=pltpu.CompilerParams(
            dimension_semantics=("arbitrary", "arbitrary", "arbitrary"), vmem_limit_bytes=VMEM_LIMIT),
        name="na_attn",
    )(proj, proj, proj, proj, proj, proj, proj, bias)


def _gqa_body(q_ref, k_ref, v_ref, o_ref, *, tk, nk):
    tq = q_ref.shape[0]
    dn = (((1,), (1,)), ((), ()))
    for g in range(GQA_GROUP):
        sl = slice(g * HEAD_DIM, (g + 1) * HEAD_DIM)
        q = q_ref[:, sl]

        def step(c, carry, q=q):
            m, l, acc = carry
            off = pl.multiple_of(c * tk, tk)
            s = lax.dot_general(q, k_ref[pl.ds(off, tk), :], dn, preferred_element_type=F32)
            m_new = jnp.maximum(m, jnp.max(s, axis=-1, keepdims=True))
            alpha = jnp.exp(m - m_new)
            p = jnp.exp(s - m_new)
            l = alpha * l + jnp.sum(p, axis=-1, keepdims=True)
            acc = alpha * acc + jnp.dot(p.astype(BF16), v_ref[pl.ds(off, tk), :], preferred_element_type=F32)
            return m_new, l, acc

        init = (jnp.full((tq, 1), -jnp.inf, F32), jnp.zeros((tq, 1), F32), jnp.zeros((tq, HEAD_DIM), F32))
        _, l, acc = lax.fori_loop(0, nk, step, init)
        o_ref[:, sl] = (acc / l).astype(o_ref.dtype)


def _gqa_attention(proj, B, L):
    T = proj.shape[0]
    tq = _tile(L, 256)
    tk = _tile(L, 512)
    nq = L // tq
    gw = GQA_GROUP * HEAD_DIM
    return pl.pallas_call(
        functools.partial(_gqa_body, tk=tk, nk=L // tk),
        grid=(B, N_KV_GQA, nq),
        in_specs=[
            pl.BlockSpec((tq, gw), lambda b, kv, qi: (b * nq + qi, COL_QB // GQA_GROUP + kv)),
            pl.BlockSpec((L, HEAD_DIM), lambda b, kv, qi: (b, COL_KB + kv)),
            pl.BlockSpec((L, HEAD_DIM), lambda b, kv, qi: (b, COL_VB + kv)),
        ],
        out_specs=pl.BlockSpec((tq, gw), lambda b, kv, qi: (b * nq + qi, kv)),
        out_shape=jax.ShapeDtypeStruct((T, WIDTH_GQA), BF16),
        compiler_params=pltpu.CompilerParams(
            dimension_semantics=("arbitrary", "arbitrary", "arbitrary"), vmem_limit_bytes=VMEM_LIMIT),
        name="gqa_attn",
    )(proj, proj, proj)


def _out_proj_body(oa_ref, ob_ref, x_ref, ga_ref, gb_ref, w_ref, g2_ref, wr_ref, br_ref,
                   x1_ref, eid_ref, gate_ref):
    mix = jnp.concatenate(
        [_rms(oa_ref[...].astype(F32), ga_ref[...]).astype(BF16),
         _rms(ob_ref[...].astype(F32), gb_ref[...]).astype(BF16)], axis=1)
    x1 = x_ref[...] + jnp.dot(mix, w_ref[...], preferred_element_type=F32)
    x1_ref[...] = x1

    h2 = _rms(x1, g2_ref[...])
    h_hi = h2.astype(BF16)
    h_lo = (h2 - h_hi.astype(F32)).astype(BF16)
    a = (jnp.dot(h_hi, wr_ref[...], preferred_element_type=F32)
         + jnp.dot(h_lo, wr_ref[...], preferred_element_type=F32) + br_ref[...])
    at = a.T
    lt = at[:ROUTER_HALF] + at[ROUTER_HALF:]

    lg = [lt[g:g + 1] for g in range(N_GROUPS)]
    gmax = functools.reduce(jnp.maximum, lg)
    grp = jnp.full_like(gmax, float(N_GROUPS))
    for g in reversed(range(N_GROUPS)):
        grp = jnp.where(lg[g] == gmax, float(g), grp)
    denom = functools.reduce(lambda u, v: u + v, [jnp.exp(x - gmax) for x in lg])
    p_grp = 1.0 / denom

    le = jnp.zeros((EXPERTS_PER_GROUP, lt.shape[1]), F32)
    for g in range(N_GROUPS):
        r0 = ROUTER_EXPERT_ROW0 + g * EXPERTS_PER_GROUP
        le = jnp.where(grp == g, lt[r0:r0 + EXPERTS_PER_GROUP], le)
    idx = lax.broadcasted_iota(jnp.int32, le.shape, 0).astype(F32)
    none = float(EXPERTS_PER_GROUP)
    v1 = jnp.max(le, axis=0, keepdims=True)
    i1 = jnp.min(jnp.where(le == v1, idx, none), axis=0, keepdims=True)
    rest = jnp.where(idx == i1, -jnp.inf, le)
    v2 = jnp.max(rest, axis=0, keepdims=True)
    i2 = jnp.min(jnp.where(rest == v2, idx, none), axis=0, keepdims=True)
    e2 = jnp.exp(v2 - v1)
    den = 1.0 + e2
    base = grp * float(EXPERTS_PER_GROUP)
    eid_ref[0:1, :] = (base + i1).astype(jnp.int32)
    eid_ref[1:2, :] = (base + i2).astype(jnp.int32)
    gate_ref[0:1, :] = p_grp * (1.0 / den)
    gate_ref[1:2, :] = p_grp * (e2 / den)


def _router_weights(w_rg, b_rg, w_re, b_re):
    D = w_rg.shape[0]
    w = jnp.zeros((D, ROUTER_HALF), F32)
    w = w.at[:, :N_GROUPS].set(w_rg.astype(F32))
    w_e = jnp.transpose(w_re.astype(F32), (1, 0, 2)).reshape(D, N_EXPERTS)
    w = w.at[:, ROUTER_EXPERT_ROW0:ROUTER_EXPERT_ROW0 + N_EXPERTS].set(w_e)
    hi = w.astype(BF16)
    lo = (w - hi.astype(F32)).astype(BF16)
    b = jnp.zeros((1, ROUTER_LANES), F32)
    b = b.at[0, :N_GROUPS].set(b_rg.astype(F32))
    b = b.at[0, ROUTER_EXPERT_ROW0:ROUTER_EXPERT_ROW0 + N_EXPERTS].set(b_re.astype(F32).reshape(-1))
    return jnp.concatenate([hi, lo], axis=1), b


def _out_proj(o_a, o_b, x2, g_a, g_b, w_bf, g2, wr, br):
    T, D = x2.shape
    tm = _tile(T, 256)
    row = lambda n: pl.BlockSpec((tm, n), lambda i: (i, 0))
    vec = lambda n: pl.BlockSpec((1, n), lambda i: (0, 0))
    full = lambda a: pl.BlockSpec(a.shape, lambda i: (0, 0))
    return pl.pallas_call(
        _out_proj_body,
        grid=(T // tm,),
        in_specs=[row(WIDTH_NA), row(WIDTH_GQA), row(D), vec(WIDTH_NA), vec(WIDTH_GQA), full(w_bf),
                  vec(D), full(wr), vec(ROUTER_LANES)],
        out_specs=[row(D), pl.BlockSpec((TOP_K, tm), lambda i: (0, i)), pl.BlockSpec((TOP_K, tm), lambda i: (0, i))],
        out_shape=[jax.ShapeDtypeStruct((T, D), F32), jax.ShapeDtypeStruct((TOP_K, T), jnp.int32),
                   jax.ShapeDtypeStruct((TOP_K, T), F32)],
        compiler_params=pltpu.CompilerParams(dimension_semantics=("arbitrary",), vmem_limit_bytes=VMEM_LIMIT),
        name="out_proj",
    )(o_a, o_b, x2, g_a.reshape(1, -1), g_b.reshape(1, -1), w_bf, g2.reshape(1, D), wr, br)


def _row_gather(src_hbm, idx_ref, dst, sem, n):
    def issue(r, carry):
        pltpu.make_async_copy(src_hbm.at[pl.ds(idx_ref[0, 0, r], 1)], dst.at[pl.ds(r, 1)], sem).start()
        return carry
    lax.fori_loop(0, n, issue, 0)


def _row_gather_wait(src_hbm, dst, sem, n):
    def wait(r, carry):
        pltpu.make_async_copy(src_hbm.at[pl.ds(0, 1)], dst.at[pl.ds(r, 1)], sem).wait()
        return carry
    lax.fori_loop(0, n, wait, 0)


def _experts_body(bexp_ref, nused_ref, tok_ref, tok_next_ref, gate_ref, x1_hbm, g2_ref, wg_ref, wu_ref, wd_ref,
                  o_ref, xbuf, sem, *, blk):
    i = pl.program_id(0)
    nused = nused_ref[0]
    slot = i % 2

    @pl.when(i == 0)
    def _():
        _row_gather(x1_hbm, tok_ref, xbuf.at[0], sem.at[0], blk)

    @pl.when(i + 1 < nused)
    def _():
        _row_gather(x1_hbm, tok_next_ref, xbuf.at[1 - slot], sem.at[1 - slot], blk)

    @pl.when(i < nused)
    def _():
        _row_gather_wait(x1_hbm, xbuf.at[slot], sem.at[slot], blk)
        h = _rms(xbuf[slot], g2_ref[...]).astype(BF16)
        g = jnp.dot(h, wg_ref[0], preferred_element_type=F32)
        u = jnp.dot(h, wu_ref[0], preferred_element_type=F32)
        act = (g * (1.0 / (1.0 + jnp.exp(-g))) * u).astype(BF16)
        y = jnp.dot(act, wd_ref[0], preferred_element_type=F32)
        o_ref[...] = y * gate_ref[...]

    @pl.when(i >= nused)
    def _():
        o_ref[...] = jnp.zeros_like(o_ref)


def _experts(x1, g2, wg, wu, wd, block_expert, nused, row_tok, row_gate, blk):
    T, D = x1.shape
    n_blocks = block_expert.shape[0]
    tok3 = row_tok.reshape(n_blocks, 1, blk)
    smem_blk = lambda shift: pl.BlockSpec(
        (1, 1, blk), lambda i, be, nu: (jnp.minimum(i + shift, n_blocks - 1), 0, 0), memory_space=pltpu.SMEM)
    wspec = lambda a: pl.BlockSpec((1,) + a.shape[1:], lambda i, be, nu: (be[i], 0, 0))
    grid_spec = pltpu.PrefetchScalarGridSpec(
        num_scalar_prefetch=2,
        grid=(n_blocks,),
        in_specs=[
            smem_blk(0), smem_blk(1),
            pl.BlockSpec((blk, 1), lambda i, be, nu: (i, 0)),
            pl.BlockSpec(memory_space=pl.ANY),
            pl.BlockSpec((1, D), lambda i, be, nu: (0, 0)),
            wspec(wg), wspec(wu), wspec(wd),
        ],
        out_specs=pl.BlockSpec((blk, D), lambda i, be, nu: (i, 0)),
        scratch_shapes=[pltpu.VMEM((2, blk, D), F32), pltpu.SemaphoreType.DMA((2,))],
    )
    return pl.pallas_call(
        functools.partial(_experts_body, blk=blk),
        grid_spec=grid_spec,
        out_shape=jax.ShapeDtypeStruct((n_blocks * blk, D), F32),
        compiler_params=pltpu.CompilerParams(dimension_semantics=("arbitrary",), vmem_limit_bytes=VMEM_LIMIT),
        name="experts",
    )(block_expert, nused, tok3, tok3, row_gate.reshape(-1, 1), x1, g2.reshape(1, D), wg, wu, wd)


def _combine_body(d_ref, d_next_ref, x1_ref, yb_hbm, gf_ref, o_ref, ybuf, sem, *, tm, n_tiles):
    i = pl.program_id(0)
    slot = i % 2

    @pl.when(i == 0)
    def _():
        _row_gather(yb_hbm, d_ref, ybuf.at[0], sem.at[0], TOP_K * tm)

    @pl.when(i + 1 < n_tiles)
    def _():
        _row_gather(yb_hbm, d_next_ref, ybuf.at[1 - slot], sem.at[1 - slot], TOP_K * tm)

    _row_gather_wait(yb_hbm, ybuf.at[slot], sem.at[slot], TOP_K * tm)
    y = x1_ref[...] + (ybuf[slot, :tm] + ybuf[slot, tm:])
    o_ref[...] = _rms(y, gf_ref[...])


def _combine(x1, yb, dest_km, g_f, tm):
    T, D = x1.shape
    n_tiles = T // tm
    smem_blk = lambda shift: pl.BlockSpec(
        (1, 1, TOP_K * tm), lambda i: (jnp.minimum(i + shift, n_tiles - 1), 0, 0), memory_space=pltpu.SMEM)
    return pl.pallas_call(
        functools.partial(_combine_body, tm=tm, n_tiles=n_tiles),
        grid=(n_tiles,),
        in_specs=[
            smem_blk(0), smem_blk(1),
            pl.BlockSpec((tm, D), lambda i: (i, 0)),
            pl.BlockSpec(memory_space=pl.ANY),
            pl.BlockSpec((1, D), lambda i: (0, 0)),
        ],
        out_specs=pl.BlockSpec((tm, D), lambda i: (i, 0)),
        out_shape=jax.ShapeDtypeStruct((T, D), F32),
        scratch_shapes=[pltpu.VMEM((2, TOP_K * tm, D), F32), pltpu.SemaphoreType.DMA((2,))],
        compiler_params=pltpu.CompilerParams(dimension_semantics=("arbitrary",), vmem_limit_bytes=VMEM_LIMIT),
        name="combine",
    )(dest_km, dest_km, x1, yb, g_f.reshape(1, D))


def _dispatch_plan(eid, gate, blk):
    T = eid.shape[1]
    A = T * TOP_K
    e_flat = eid.T.reshape(A)
    onehot = (e_flat[:, None] == jnp.arange(N_EXPERTS)[None, :]).astype(jnp.int32)
    csum = jnp.cumsum(onehot, axis=0)
    counts = csum[-1]
    rank = jnp.sum((csum - onehot) * onehot, axis=1)
    padded = (counts + blk - 1) // blk * blk
    pad_end = jnp.cumsum(padded)
    pad_start = pad_end - padded
    dest = (pad_start[e_flat] + rank).astype(jnp.int32)
    n_blocks = A // blk + N_EXPERTS
    n_rows = n_blocks * blk
    tok = (jnp.arange(A) // TOP_K).astype(jnp.int32)
    row_tok = jnp.zeros((n_rows,), jnp.int32).at[dest].set(tok)
    row_gate = jnp.zeros((n_rows,), F32).at[dest].set(gate.T.reshape(A))
    block_expert = jnp.minimum(
        jnp.searchsorted(pad_end, jnp.arange(n_blocks) * blk, side="right"), N_EXPERTS - 1).astype(jnp.int32)
    nused = (pad_end[-1] // blk).astype(jnp.int32).reshape(1)
    return dest, row_tok, row_gate, block_expert, nused


def _trunk(x, p, moe_blk=256):
    B, L, D = x.shape
    T = B * L
    x2 = x.reshape(T, D)
    proj = _in_proj(x2, p["g1"], p["w_in"], _rope_tables(L), p["g_q"], p["g_k"], L)
    o_a = _na_attention(proj, p["na_bias"], B, L)
    o_b = _gqa_attention(proj, B, L)
    x1, eid, gate = _out_proj(o_a, o_b, x2, p["g_out_a"], p["g_out_b"], p["w_out"], p["g2"], p["wr"], p["br"])
    dest, row_tok, row_gate, block_expert, nused = _dispatch_plan(eid, gate, moe_blk)
    yb = _experts(x1, p["g2"], p["w_gate"], p["w_up"], p["w_down"], block_expert, nused, row_tok, row_gate, moe_blk)
    tm = _tile(T, 256)
    dest_km = dest.reshape(T // tm, tm, TOP_K).transpose(0, 2, 1).reshape(T // tm, 1, TOP_K * tm)
    y = _combine(x1, yb, dest_km, p["g_f"], tm)
    return y.reshape(B, L, D)


def _prepare(g_norm1, w_in, rpb, g_q, g_k, g_out_a, g_out_b, w_out, g_norm2, w_router_group, b_router_group,
             w_router_expert, b_router_expert, w_gate, w_up, w_down, g_norm_f):
    wr, br = _router_weights(w_router_group, b_router_group, w_router_expert, b_router_expert)
    return dict(
        g1=g_norm1, w_in=w_in.astype(BF16), na_bias=_na_bias_tables(rpb), g_q=g_q, g_k=g_k,
        g_out_a=g_out_a, g_out_b=g_out_b, w_out=w_out.astype(BF16), g2=g_norm2, wr=wr, br=br,
        w_gate=w_gate.astype(BF16), w_up=w_up.astype(BF16), w_down=w_down.astype(BF16), g_f=g_norm_f)


def kernel(x_prompt, x_sample, g_norm1, w_in, rpb, g_q, g_k, g_out_a, g_out_b, w_out, g_norm2, w_router_group,
           b_router_group, w_router_expert, b_router_expert, w_gate, w_up, w_down, g_norm_f):
    assert g_norm1.shape[0] == 1, "single-layer trunk"
    p = _prepare(g_norm1[0], w_in[0], rpb[0], g_q[0], g_k[0], g_out_a[0], g_out_b[0], w_out[0], g_norm2[0],
                 w_router_group[0], b_router_group[0], w_router_expert[0], b_router_expert[0],
                 w_gate[0], w_up[0], w_down[0], g_norm_f)
    return (_trunk(x_prompt, p), _trunk(x_sample, p))
```

```python
import functools

import numpy as np
import jax
import jax.numpy as jnp
from jax import lax
from jax.experimental import pallas as pl
from jax.experimental.pallas import tpu as pltpu

F32 = jnp.float32
BF16 = jnp.bfloat16

D_MODEL = 2048
HEAD_DIM = 128
N_HEADS_NA = 8
N_HEADS_GQA = 8
N_KV_GQA = 2
GQA_GROUP = N_HEADS_GQA // N_KV_GQA
WIDTH_NA = N_HEADS_NA * HEAD_DIM
WIDTH_GQA = N_HEADS_GQA * HEAD_DIM
KV_WIDTH_GQA = N_KV_GQA * HEAD_DIM
IN_COLS = 3 * WIDTH_NA + WIDTH_GQA + 2 * KV_WIDTH_GQA
GRID_W = 64
NA_KH = 8
NA_KW = 16
ROPE_THETA = 10000.0
AXIS_DIM = HEAD_DIM // 2
N_GROUPS = 4
EXPERTS_PER_GROUP = 8
N_EXPERTS = N_GROUPS * EXPERTS_PER_GROUP
TOP_K = 2
D_EXPERT = 512
EPS = 1e-6
NEG = -1e30
LOG2E = 1.4426950408889634

COL_QA = 0
COL_KA = N_HEADS_NA
COL_VA = 2 * N_HEADS_NA
COL_QB = 3 * N_HEADS_NA
COL_KB = COL_QB + N_HEADS_GQA
COL_VB = COL_KB + N_KV_GQA

NA_QROWS = 4
NA_QB = NA_QROWS * GRID_W
NA_SEG = 3
PROJ_TN = 512
ROUTER_LANES = 128
ROUTER_HALF = 64
ROUTER_EXPERT_ROW0 = 8
VMEM_LIMIT = 56 * 1024 * 1024
GATHER_UNROLL = 8


def _tile(n, pref):
    t = min(n, pref)
    while n % t:
        t -= 1
    return t


def _rms(x, g):
    ms = jnp.mean(x * x, axis=-1, keepdims=True)
    return x * lax.rsqrt(ms + EPS) * g


def _in_proj_body(x_ref, g1_ref, w_ref, cos_ref, sa_ref, sb_ref, gq_ref, gk_ref, o_ref, h_scr):
    j = pl.program_id(1)

    @pl.when(j == 0)
    def _():
        h_scr[...] = _rms(x_ref[...], g1_ref[...]).astype(BF16)

    acc = jnp.dot(h_scr[...], w_ref[...], preferred_element_type=F32)
    scale = HEAD_DIM ** -0.5
    heads_per_tile = PROJ_TN // HEAD_DIM

    def rope(u, g, post):
        y = _rms(u, g)
        r = (y * cos_ref[...]
             + pltpu.roll(y, HEAD_DIM - AXIS_DIM // 2, axis=1) * sa_ref[...]
             + pltpu.roll(y, AXIS_DIM // 2, axis=1) * sb_ref[...])
        return r * post

    n_qa = WIDTH_NA // PROJ_TN
    n_plain_end = 3 * WIDTH_NA // PROJ_TN
    n_qb_end = n_plain_end + WIDTH_GQA // PROJ_TN

    @pl.when(j < n_qa)
    def _():
        o_ref[...] = (acc * scale).astype(o_ref.dtype)

    @pl.when((j >= n_qa) & (j < n_plain_end))
    def _():
        o_ref[...] = acc.astype(o_ref.dtype)

    @pl.when((j >= n_plain_end) & (j < n_qb_end))
    def _():
        for h in range(heads_per_tile):
            sl = slice(h * HEAD_DIM, (h + 1) * HEAD_DIM)
            o_ref[:, sl] = rope(acc[:, sl], gq_ref[...], scale * LOG2E).astype(o_ref.dtype)

    @pl.when(j >= n_qb_end)
    def _():
        for h in range(heads_per_tile):
            sl = slice(h * HEAD_DIM, (h + 1) * HEAD_DIM)
            if h < N_KV_GQA:
                o_ref[:, sl] = rope(acc[:, sl], gk_ref[...], 1.0).astype(o_ref.dtype)
            else:
                o_ref[:, sl] = acc[:, sl].astype(o_ref.dtype)


def _rope_tables(L):
    t = jnp.arange(L)
    row = (t // GRID_W).astype(F32)
    col = (t % GRID_W).astype(F32)
    inv = ROPE_THETA ** (-jnp.arange(0, AXIS_DIM, 2, dtype=F32) / AXIS_DIM)
    ang_r = row[:, None] * inv
    ang_c = col[:, None] * inv
    ang = jnp.concatenate([ang_r, ang_r, ang_c, ang_c], axis=-1)
    cos, sin = jnp.cos(ang), jnp.sin(ang)
    first_half = (np.arange(HEAD_DIM) % AXIS_DIM) < AXIS_DIM // 2
    sa = jnp.where(first_half, -sin, 0.0)
    sb = jnp.where(first_half, 0.0, sin)
    return cos, sa, sb


def _in_proj(x2, g1, w_bf, tables, gq, gk, L):
    T, D = x2.shape
    tm = _tile(L, 1024)
    n_pos = L // tm
    cos, sa, sb = tables
    assert IN_COLS % PROJ_TN == 0 and (IN_COLS - 2 * KV_WIDTH_GQA) % PROJ_TN == 0
    pos_spec = pl.BlockSpec((tm, HEAD_DIM), lambda i, j: (i % n_pos, 0))
    vec = lambda n: pl.BlockSpec((1, n), lambda i, j: (0, 0))
    return pl.pallas_call(
        _in_proj_body,
        grid=(T // tm, IN_COLS // PROJ_TN),
        in_specs=[
            pl.BlockSpec((tm, D), lambda i, j: (i, 0)),
            vec(D),
            pl.BlockSpec((D, PROJ_TN), lambda i, j: (0, j)),
            pos_spec, pos_spec, pos_spec,
            vec(HEAD_DIM), vec(HEAD_DIM),
        ],
        out_specs=pl.BlockSpec((tm, PROJ_TN), lambda i, j: (i, j)),
        out_shape=jax.ShapeDtypeStruct((T, IN_COLS), BF16),
        scratch_shapes=[pltpu.VMEM((tm, D), BF16)],
        compiler_params=pltpu.CompilerParams(
            dimension_semantics=("arbitrary", "arbitrary"), vmem_limit_bytes=VMEM_LIMIT),
        name="in_proj",
    )(x2, g1.reshape(1, D), w_bf, cos, sa, sb, gq.reshape(1, HEAD_DIM), gk.reshape(1, HEAD_DIM))


def _na_bias_tables(rpb):
    a = np.arange(NA_QROWS)[:, None, None, None, None]
    c = np.arange(GRID_W)[None, :, None, None, None]
    s = np.arange(NA_SEG)[None, None, :, None, None]
    j = np.arange(NA_QROWS)[None, None, None, :, None]
    kc = np.arange(GRID_W)[None, None, None, None, :]
    rel = NA_QROWS * (s - 1) + j
    dr = rel - a
    half = NA_KH // 2
    row_ok = np.stack([
        (rel >= 0) & (rel < NA_KH) & (a >= 0),
        (dr >= -half) & (dr < NA_KH - half),
        (rel >= NA_QROWS - NA_KH) & (rel < NA_QROWS) & (a >= 0),
    ])
    ws = np.clip(c - NA_KW // 2, 0, GRID_W - NA_KW)
    col_ok = (kc >= ws) & (kc < ws + NA_KW)
    ok = np.broadcast_to(row_ok & col_ok, (3, NA_QROWS, GRID_W, NA_SEG, NA_QROWS, GRID_W))
    ok = ok.reshape(3, NA_QB, NA_SEG * NA_QB)
    ridx = np.clip(dr + NA_KH - 1, 0, 2 * NA_KH - 2).reshape(NA_QROWS, NA_SEG * NA_QROWS)
    rsel = rpb.astype(F32)[:, ridx, :]
    cidx = np.clip(kc - c + NA_KW - 1, 0, 2 * NA_KW - 2).reshape(GRID_W, GRID_W)
    onehot = (np.arange(2 * NA_KW - 1)[:, None, None] == cidx[None]).astype(np.float32)
    vals = jnp.einsum("haxi,ick->hacxk", rsel, onehot, precision=lax.Precision.HIGHEST)
    vals = vals.reshape(rpb.shape[0], NA_QB, NA_SEG * NA_QB)
    return jnp.where(ok[None], vals[:, None], NEG)


def _na_body(q_ref, k0_ref, k1_ref, k2_ref, v0_ref, v1_ref, v2_ref, b_ref, o_ref, *, nblk):
    i = pl.program_id(2)
    variant = jnp.where(i == 0, 0, jnp.where(i == nblk - 1, 2, 1))
    q = q_ref[...]
    dn = (((1,), (1,)), ((), ()))
    s = jnp.concatenate(
        [lax.dot_general(q, k[...], dn, preferred_element_type=F32) for k in (k0_ref, k1_ref, k2_ref)],
        axis=1)
    s = s + b_ref[0, variant]
    m = jnp.max(s, axis=-1, keepdims=True)
    p = jnp.exp(s - m)
    l = jnp.sum(p, axis=-1, keepdims=True)
    pb = p.astype(BF16)
    o = jnp.zeros((NA_QB, HEAD_DIM), F32)
    for seg, v in enumerate((v0_ref, v1_ref, v2_ref)):
        o = o + jnp.dot(pb[:, seg * NA_QB:(seg + 1) * NA_QB], v[...], preferred_element_type=F32)
    o_ref[...] = (o / l).astype(o_ref.dtype)


def _na_attention(proj, bias, B, L):
    T = proj.shape[0]
    rows = L // GRID_W
    assert L % GRID_W == 0 and rows % NA_QROWS == 0 and rows >= NA_KH
    nblk = rows // NA_QROWS

    def kv_spec(col0, shift):
        def imap(h, b, i):
            return (b * nblk + jnp.clip(i + shift, 0, nblk - 1), col0 + h)
        return pl.BlockSpec((NA_QB, HEAD_DIM), imap)

    return pl.pallas_call(
        functools.partial(_na_body, nblk=nblk),
        grid=(N_HEADS_NA, B, nblk),
        in_specs=[
            pl.BlockSpec((NA_QB, HEAD_DIM), lambda h, b, i: (b * nblk + i, COL_QA + h)),
            kv_spec(COL_KA, -1), kv_spec(COL_KA, 0), kv_spec(COL_KA, 1),
            kv_spec(COL_VA, -1), kv_spec(COL_VA, 0), kv_spec(COL_VA, 1),
            pl.BlockSpec((1, 3, NA_QB, NA_SEG * NA_QB), lambda h, b, i: (h, 0, 0, 0)),
        ],
        out_specs=pl.BlockSpec((NA_QB, HEAD_DIM), lambda h, b, i: (b * nblk + i, h)),
        out_shape=jax.ShapeDtypeStruct((T, WIDTH_NA), BF16),
        compiler_params=pltpu.CompilerParams(
            dimension_semantics=("arbitrary", "arbitrary", "arbitrary"), vmem_limit_bytes=VMEM_LIMIT),
        name="na_attn",
    )(proj, proj, proj, proj, proj, proj, proj, bias)


def _gqa_body(q_ref, k_ref, v_ref, o_ref, v1_scr, q_scr, s0_scr, s1_scr, p0_scr, p1_scr, al0_scr, al1_scr,
              m_scr, acc_scr, *, tk, nk):
    tq = q_ref.shape[0]
    L = k_ref.shape[0]
    dn = (((1,), (1,)), ((), ()))

    @pl.when(pl.program_id(2) == 0)
    def _():
        v1_scr[:, :HEAD_DIM] = v_ref[...]
        v1_scr[:, HEAD_DIM:] = jnp.ones((L, HEAD_DIM), BF16)

    for g in range(GQA_GROUP):
        q_scr[g * tq:(g + 1) * tq, :] = q_ref[:, g * HEAD_DIM:(g + 1) * HEAD_DIM]
    m_scr[...] = jnp.full(m_scr.shape, -jnp.inf, F32)
    acc_scr[...] = jnp.zeros(acc_scr.shape, F32)

    def scores(c, s_scr):
        off = pl.multiple_of(c * tk, tk)
        s_scr[...] = lax.dot_general(q_scr[...], k_ref[pl.ds(off, tk), :], dn, preferred_element_type=F32)

    def softmax(s_scr, p_scr, al_scr):
        s = s_scr[...]
        m_prev = m_scr[...]
        m_new = jnp.maximum(m_prev, jnp.max(s, axis=-1, keepdims=True))
        al_scr[...] = jnp.exp2(m_prev - m_new)
        p_scr[...] = jnp.exp2(s - jnp.concatenate([m_new] * (tk // HEAD_DIM), axis=1)).astype(BF16)
        m_scr[...] = m_new

    def weighted_values(c, p_scr, al_scr):
        off = pl.multiple_of(c * tk, tk)
        pv = jnp.dot(p_scr[...], v1_scr[pl.ds(off, tk), :], preferred_element_type=F32)
        al = al_scr[...]
        acc_scr[...] = jnp.concatenate([al, al], axis=1) * acc_scr[...] + pv

    scores(0, s0_scr)
    scores(1, s1_scr)
    softmax(s0_scr, p0_scr, al0_scr)

    def pair(i2, carry):
        i = 2 + 2 * i2
        scores(i, s0_scr)
        softmax(s1_scr, p1_scr, al1_scr)
        weighted_values(i - 2, p0_scr, al0_scr)
        scores(i + 1, s1_scr)
        softmax(s0_scr, p0_scr, al0_scr)
        weighted_values(i - 1, p1_scr, al1_scr)
        return carry

    lax.fori_loop(0, (nk - 2) // 2, pair, 0)
    softmax(s1_scr, p1_scr, al1_scr)
    weighted_values(nk - 2, p0_scr, al0_scr)
    weighted_values(nk - 1, p1_scr, al1_scr)
    o = acc_scr[:, :HEAD_DIM] / acc_scr[:, HEAD_DIM:]
    for g in range(GQA_GROUP):
        o_ref[:, g * HEAD_DIM:(g + 1) * HEAD_DIM] = o[g * tq:(g + 1) * tq].astype(o_ref.dtype)


def _gqa_attention(proj, B, L):
    T = proj.shape[0]
    tq = _tile(L, 256)
    tk = _tile(L, 1024 if L >= 8192 else 512)
    nk = L // tk
    assert tk % HEAD_DIM == 0 and nk >= 2 and nk % 2 == 0
    nq = L // tq
    gw = GQA_GROUP * HEAD_DIM
    rows = GQA_GROUP * tq
    vm = pltpu.VMEM
    return pl.pallas_call(
        functools.partial(_gqa_body, tk=tk, nk=nk),
        grid=(B, N_KV_GQA, nq),
        in_specs=[
            pl.BlockSpec((tq, gw), lambda b, kv, qi: (b * nq + qi, COL_QB // GQA_GROUP + kv)),
            pl.BlockSpec((L, HEAD_DIM), lambda b, kv, qi: (b, COL_KB + kv)),
            pl.BlockSpec((L, HEAD_DIM), lambda b, kv, qi: (b, COL_VB + kv)),
        ],
        out_specs=pl.BlockSpec((tq, gw), lambda b, kv, qi: (b * nq + qi, kv)),
        out_shape=jax.ShapeDtypeStruct((T, WIDTH_GQA), BF16),
        scratch_shapes=[
            vm((L, 2 * HEAD_DIM), BF16), vm((rows, HEAD_DIM), BF16),
            vm((rows, tk), F32), vm((rows, tk), F32), vm((rows, tk), BF16), vm((rows, tk), BF16),
            vm((rows, HEAD_DIM), F32), vm((rows, HEAD_DIM), F32),
            vm((rows, HEAD_DIM), F32), vm((rows, 2 * HEAD_DIM), F32)],
        compiler_params=pltpu.CompilerParams(
            dimension_semantics=("arbitrary", "arbitrary", "arbitrary"), vmem_limit_bytes=VMEM_LIMIT),
        name="gqa_attn",
    )(proj, proj, proj)


def _out_proj_body(oa_ref, ob_ref, x_ref, ga_ref, gb_ref, w_ref, g2_ref, wr_ref, br_ref,
                   x1_ref, eid_ref, gate_ref):
    mix = jnp.concatenate(
        [_rms(oa_ref[...].astype(F32), ga_ref[...]).astype(BF16),
         _rms(ob_ref[...].astype(F32), gb_ref[...]).astype(BF16)], axis=1)
    x1 = x_ref[...] + jnp.dot(mix, w_ref[...], preferred_element_type=F32)
    x1_ref[...] = x1

    h2 = _rms(x1, g2_ref[...])
    h_hi = h2.astype(BF16)
    h_lo = (h2 - h_hi.astype(F32)).astype(BF16)
    a = (jnp.dot(h_hi, wr_ref[...], preferred_element_type=F32)
         + jnp.dot(h_lo, wr_ref[...], preferred_element_type=F32) + br_ref[...])
    at = a.T
    lt = at[:ROUTER_HALF] + at[ROUTER_HALF:]

    lg = [lt[g:g + 1] for g in range(N_GROUPS)]
    gmax = functools.reduce(jnp.maximum, lg)
    grp = jnp.full_like(gmax, float(N_GROUPS))
    for g in reversed(range(N_GROUPS)):
        grp = jnp.where(lg[g] == gmax, float(g), grp)
    denom = functools.reduce(lambda u, v: u + v, [jnp.exp(x - gmax) for x in lg])
    p_grp = 1.0 / denom

    le = jnp.zeros((EXPERTS_PER_GROUP, lt.shape[1]), F32)
    for g in range(N_GROUPS):
        r0 = ROUTER_EXPERT_ROW0 + g * EXPERTS_PER_GROUP
        le = jnp.where(grp == g, lt[r0:r0 + EXPERTS_PER_GROUP], le)
    idx = lax.broadcasted_iota(jnp.int32, le.shape, 0).astype(F32)
    none = float(EXPERTS_PER_GROUP)
    v1 = jnp.max(le, axis=0, keepdims=True)
    i1 = jnp.min(jnp.where(le == v1, idx, none), axis=0, keepdims=True)
    rest = jnp.where(idx == i1, -jnp.inf, le)
    v2 = jnp.max(rest, axis=0, keepdims=True)
    i2 = jnp.min(jnp.where(rest == v2, idx, none), axis=0, keepdims=True)
    e2 = jnp.exp(v2 - v1)
    den = 1.0 + e2
    base = grp * float(EXPERTS_PER_GROUP)
    eid_ref[0:1, :] = (base + i1).astype(jnp.int32)
    eid_ref[1:2, :] = (base + i2).astype(jnp.int32)
    gate_ref[0:1, :] = p_grp * (1.0 / den)
    gate_ref[1:2, :] = p_grp * (e2 / den)


def _router_weights(w_rg, b_rg, w_re, b_re):
    D = w_rg.shape[0]
    w = jnp.zeros((D, ROUTER_HALF), F32)
    w = w.at[:, :N_GROUPS].set(w_rg.astype(F32))
    w_e = jnp.transpose(w_re.astype(F32), (1, 0, 2)).reshape(D, N_EXPERTS)
    w = w.at[:, ROUTER_EXPERT_ROW0:ROUTER_EXPERT_ROW0 + N_EXPERTS].set(w_e)
    hi = w.astype(BF16)
    lo = (w - hi.astype(F32)).astype(BF16)
    b = jnp.zeros((1, ROUTER_LANES), F32)
    b = b.at[0, :N_GROUPS].set(b_rg.astype(F32))
    b = b.at[0, ROUTER_EXPERT_ROW0:ROUTER_EXPERT_ROW0 + N_EXPERTS].set(b_re.astype(F32).reshape(-1))
    return jnp.concatenate([hi, lo], axis=1), b


def _out_proj(o_a, o_b, x2, g_a, g_b, w_bf, g2, wr, br):
    T, D = x2.shape
    tm = _tile(T, 256)
    row = lambda n: pl.BlockSpec((tm, n), lambda i: (i, 0))
    vec = lambda n: pl.BlockSpec((1, n), lambda i: (0, 0))
    full = lambda a: pl.BlockSpec(a.shape, lambda i: (0, 0))
    return pl.pallas_call(
        _out_proj_body,
        grid=(T // tm,),
        in_specs=[row(WIDTH_NA), row(WIDTH_GQA), row(D), vec(WIDTH_NA), vec(WIDTH_GQA), full(w_bf),
                  vec(D), full(wr), vec(ROUTER_LANES)],
        out_specs=[row(D), pl.BlockSpec((TOP_K, tm), lambda i: (0, i)), pl.BlockSpec((TOP_K, tm), lambda i: (0, i))],
        out_shape=[jax.ShapeDtypeStruct((T, D), F32), jax.ShapeDtypeStruct((TOP_K, T), jnp.int32),
                   jax.ShapeDtypeStruct((TOP_K, T), F32)],
        compiler_params=pltpu.CompilerParams(dimension_semantics=("arbitrary",), vmem_limit_bytes=VMEM_LIMIT),
        name="out_proj",
    )(o_a, o_b, x2, g_a.reshape(1, -1), g_b.reshape(1, -1), w_bf, g2.reshape(1, D), wr, br)


def _row_gather(src_hbm, idx_ref, dst, sem, n):
    assert n % GATHER_UNROLL == 0

    def issue(g, carry):
        for u in range(GATHER_UNROLL):
            r = g * GATHER_UNROLL + u
            pltpu.make_async_copy(src_hbm.at[pl.ds(idx_ref[0, 0, r], 1)], dst.at[pl.ds(r, 1)], sem).start(
                priority=u % 2)
        return carry
    lax.fori_loop(0, n // GATHER_UNROLL, issue, 0)


def _row_gather_wait(src_hbm, dst, sem, n):
    assert dst.shape[0] == n
    pltpu.make_async_copy(src_hbm.at[pl.ds(0, n)], dst, sem).wait()


def _experts_body(bexp_ref, nused_ref, tok_ref, tok_next_ref, gate_ref, x1_hbm, g2_ref, wg_ref, wu_ref, wd_ref,
                  o_ref, xbuf, sem, *, blk):
    i = pl.program_id(0)
    nused = nused_ref[0]
    slot = i % 2

    @pl.when(i == 0)
    def _():
        _row_gather(x1_hbm, tok_ref, xbuf.at[0], sem.at[0], blk)

    @pl.when(i + 1 < nused)
    def _():
        _row_gather(x1_hbm, tok_next_ref, xbuf.at[1 - slot], sem.at[1 - slot], blk)

    @pl.when(i < nused)
    def _():
        _row_gather_wait(x1_hbm, xbuf.at[slot], sem.at[slot], blk)
        h = _rms(xbuf[slot], g2_ref[...]).astype(BF16)
        g = jnp.dot(h, wg_ref[0], preferred_element_type=F32)
        u = jnp.dot(h, wu_ref[0], preferred_element_type=F32)
        act = (g * (1.0 / (1.0 + jnp.exp(-g))) * u).astype(BF16)
        y = jnp.dot(act, wd_ref[0], preferred_element_type=F32)
        o_ref[...] = y * gate_ref[...]

    @pl.when(i >= nused)
    def _():
        o_ref[...] = jnp.zeros_like(o_ref)


def _experts(x1, g2, wg, wu, wd, block_expert, nused, row_tok, row_gate, blk):
    T, D = x1.shape
    n_blocks = block_expert.shape[0]
    tok3 = row_tok.reshape(n_blocks, 1, blk)
    smem_blk = lambda shift: pl.BlockSpec(
        (1, 1, blk), lambda i, be, nu: (jnp.minimum(i + shift, n_blocks - 1), 0, 0), memory_space=pltpu.SMEM)
    wspec = lambda a: pl.BlockSpec((1,) + a.shape[1:], lambda i, be, nu: (be[i], 0, 0))
    grid_spec = pltpu.PrefetchScalarGridSpec(
        num_scalar_prefetch=2,
        grid=(n_blocks,),
        in_specs=[
            smem_blk(0), smem_blk(1),
            pl.BlockSpec((blk, 1), lambda i, be, nu: (i, 0)),
            pl.BlockSpec(memory_space=pl.ANY),
            pl.BlockSpec((1, D), lambda i, be, nu: (0, 0)),
            wspec(wg), wspec(wu), wspec(wd),
        ],
        out_specs=pl.BlockSpec((blk, D), lambda i, be, nu: (i, 0)),
        scratch_shapes=[pltpu.VMEM((2, blk, D), F32), pltpu.SemaphoreType.DMA((2,))],
    )
    return pl.pallas_call(
        functools.partial(_experts_body, blk=blk),
        grid_spec=grid_spec,
        out_shape=jax.ShapeDtypeStruct((n_blocks * blk, D), F32),
        compiler_params=pltpu.CompilerParams(dimension_semantics=("arbitrary",), vmem_limit_bytes=VMEM_LIMIT),
        name="experts",
    )(block_expert, nused, tok3, tok3, row_gate.reshape(-1, 1), x1, g2.reshape(1, D), wg, wu, wd)


def _combine_body(d_ref, d_next_ref, x1_ref, yb_hbm, gf_ref, o_ref, ybuf, sem, *, tm, n_tiles):
    i = pl.program_id(0)
    slot = i % 2

    @pl.when(i == 0)
    def _():
        _row_gather(yb_hbm, d_ref, ybuf.at[0], sem.at[0], TOP_K * tm)

    @pl.when(i + 1 < n_tiles)
    def _():
        _row_gather(yb_hbm, d_next_ref, ybuf.at[1 - slot], sem.at[1 - slot], TOP_K * tm)

    _row_gather_wait(yb_hbm, ybuf.at[slot], sem.at[slot], TOP_K * tm)
    y = x1_ref[...] + (ybuf[slot, :tm] + ybuf[slot, tm:])
    o_ref[...] = _rms(y, gf_ref[...])


def _combine(x1, yb, dest_km, g_f, tm):
    T, D = x1.shape
    n_tiles = T // tm
    smem_blk = lambda shift: pl.BlockSpec(
        (1, 1, TOP_K * tm), lambda i: (jnp.minimum(i + shift, n_tiles - 1), 0, 0), memory_space=pltpu.SMEM)
    return pl.pallas_call(
        functools.partial(_combine_body, tm=tm, n_tiles=n_tiles),
        grid=(n_tiles,),
        in_specs=[
            smem_blk(0), smem_blk(1),
            pl.BlockSpec((tm, D), lambda i: (i, 0)),
            pl.BlockSpec(memory_space=pl.ANY),
            pl.BlockSpec((1, D), lambda i: (0, 0)),
        ],
        out_specs=pl.BlockSpec((tm, D), lambda i: (i, 0)),
        out_shape=jax.ShapeDtypeStruct((T, D), F32),
        scratch_shapes=[pltpu.VMEM((2, TOP_K * tm, D), F32), pltpu.SemaphoreType.DMA((2,))],
        compiler_params=pltpu.CompilerParams(dimension_semantics=("arbitrary",), vmem_limit_bytes=VMEM_LIMIT),
        name="combine",
    )(dest_km, dest_km, x1, yb, g_f.reshape(1, D))


def _dispatch_plan(eid, gate, blk):
    T = eid.shape[1]
    A = T * TOP_K
    e_flat = eid.T.reshape(A)
    onehot = (e_flat[:, None] == jnp.arange(N_EXPERTS)[None, :]).astype(jnp.int32)
    csum = jnp.cumsum(onehot, axis=0)
    counts = csum[-1]
    rank = jnp.sum((csum - onehot) * onehot, axis=1)
    padded = (counts + blk - 1) // blk * blk
    pad_end = jnp.cumsum(padded)
    pad_start = pad_end - padded
    dest = (pad_start[e_flat] + rank).astype(jnp.int32)
    n_blocks = A // blk + N_EXPERTS
    n_rows = n_blocks * blk
    tok = (jnp.arange(A) // TOP_K).astype(jnp.int32)
    row_tok = jnp.zeros((n_rows,), jnp.int32).at[dest].set(tok)
    row_gate = jnp.zeros((n_rows,), F32).at[dest].set(gate.T.reshape(A))
    block_expert = jnp.minimum(
        jnp.searchsorted(pad_end, jnp.arange(n_blocks) * blk, side="right"), N_EXPERTS - 1).astype(jnp.int32)
    nused = (pad_end[-1] // blk).astype(jnp.int32).reshape(1)
    return dest, row_tok, row_gate, block_expert, nused


def _trunk(x, p, moe_blk=256):
    B, L, D = x.shape
    T = B * L
    x2 = x.reshape(T, D)
    proj = _in_proj(x2, p["g1"], p["w_in"], _rope_tables(L), p["g_q"], p["g_k"], L)
    o_a = _na_attention(proj, p["na_bias"], B, L)
    o_b = _gqa_attention(proj, B, L)
    x1, eid, gate = _out_proj(o_a, o_b, x2, p["g_out_a"], p["g_out_b"], p["w_out"], p["g2"], p["wr"], p["br"])
    dest, row_tok, row_gate, block_expert, nused = _dispatch_plan(eid, gate, moe_blk)
    yb = _experts(x1, p["g2"], p["w_gate"], p["w_up"], p["w_down"], block_expert, nused, row_tok, row_gate, moe_blk)
    tm = _tile(T, 256)
    dest_km = dest.reshape(T // tm, tm, TOP_K).transpose(0, 2, 1).reshape(T // tm, 1, TOP_K * tm)
    y = _combine(x1, yb, dest_km, p["g_f"], tm)
    return y.reshape(B, L, D)


def _prepare(g_norm1, w_in, rpb, g_q, g_k, g_out_a, g_out_b, w_out, g_norm2, w_router_group, b_router_group,
             w_router_expert, b_router_expert, w_gate, w_up, w_down, g_norm_f):
    wr, br = _router_weights(w_router_group, b_router_group, w_router_expert, b_router_expert)
    return dict(
        g1=g_norm1, w_in=w_in.astype(BF16), na_bias=_na_bias_tables(rpb), g_q=g_q, g_k=g_k,
        g_out_a=g_out_a, g_out_b=g_out_b, w_out=w_out.astype(BF16), g2=g_norm2, wr=wr, br=br,
        w_gate=w_gate.astype(BF16), w_up=w_up.astype(BF16), w_down=w_down.astype(BF16), g_f=g_norm_f)


def kernel(x_prompt, x_sample, g_norm1, w_in, rpb, g_q, g_k, g_out_a, g_out_b, w_out, g_norm2, w_router_group,
           b_router_group, w_router_expert, b_router_expert, w_gate, w_up, w_down, g_norm_f):
    assert g_norm1.shape[0] == 1, "single-layer trunk"
    p = _prepare(g_norm1[0], w_in[0], rpb[0], g_q[0], g_k[0], g_out_a[0], g_out_b[0], w_out[0], g_norm2[0],
                 w_router_group[0], b_router_group[0], w_router_expert[0], b_router_expert[0],
                 w_gate[0], w_up[0], w_down[0], g_norm_f)
    return (_trunk(x_prompt, p), _trunk(x_sample, p))
```

```python
import functools

import numpy as np
import jax
import jax.numpy as jnp
from jax import lax
from jax.experimental import pallas as pl
from jax.experimental.pallas import tpu as pltpu

F32 = jnp.float32
BF16 = jnp.bfloat16

D_MODEL = 2048
HEAD_DIM = 128
N_HEADS_NA = 8
N_HEADS_GQA = 8
N_KV_GQA = 2
GQA_GROUP = N_HEADS_GQA // N_KV_GQA
WIDTH_NA = N_HEADS_NA * HEAD_DIM
WIDTH_GQA = N_HEADS_GQA * HEAD_DIM
KV_WIDTH_GQA = N_KV_GQA * HEAD_DIM
IN_COLS = 3 * WIDTH_NA + WIDTH_GQA + 2 * KV_WIDTH_GQA
GRID_W = 64
NA_KH = 8
NA_KW = 16
ROPE_THETA = 10000.0
AXIS_DIM = HEAD_DIM // 2
N_GROUPS = 4
EXPERTS_PER_GROUP = 8
N_EXPERTS = N_GROUPS * EXPERTS_PER_GROUP
TOP_K = 2
D_EXPERT = 512
EPS = 1e-6
NEG = -1e30
LOG2E = 1.4426950408889634

COL_QA = 0
COL_KA = N_HEADS_NA
COL_VA = 2 * N_HEADS_NA
COL_QB = 3 * N_HEADS_NA
COL_KB = COL_QB + N_HEADS_GQA
COL_VB = COL_KB + N_KV_GQA

NA_QROWS = 4
NA_QB = NA_QROWS * GRID_W
NA_SEG = 3
PROJ_TN = 512
ROUTER_LANES = 128
ROUTER_HALF = 64
ROUTER_EXPERT_ROW0 = 8
VMEM_LIMIT = 56 * 1024 * 1024
GATHER_UNROLL = 8


def _tile(n, pref):
    t = min(n, pref)
    while n % t:
        t -= 1
    return t


def _rms(x, g):
    ms = jnp.mean(x * x, axis=-1, keepdims=True)
    return x * lax.rsqrt(ms + EPS) * g


def _in_proj_body(x_ref, g1_ref, w_ref, cos_ref, sa_ref, sb_ref, gq_ref, gk_ref, o_ref, h_scr):
    j = pl.program_id(1)

    @pl.when(j == 0)
    def _():
        h_scr[...] = _rms(x_ref[...], g1_ref[...]).astype(BF16)

    acc = jnp.dot(h_scr[...], w_ref[...], preferred_element_type=F32)
    scale = HEAD_DIM ** -0.5
    heads_per_tile = PROJ_TN // HEAD_DIM

    def rope(u, g, post):
        y = _rms(u, g)
        r = (y * cos_ref[...]
             + pltpu.roll(y, HEAD_DIM - AXIS_DIM // 2, axis=1) * sa_ref[...]
             + pltpu.roll(y, AXIS_DIM // 2, axis=1) * sb_ref[...])
        return r * post

    n_qa = WIDTH_NA // PROJ_TN
    n_plain_end = 3 * WIDTH_NA // PROJ_TN
    n_qb_end = n_plain_end + WIDTH_GQA // PROJ_TN

    @pl.when(j < n_qa)
    def _():
        o_ref[...] = (acc * (scale * LOG2E)).astype(o_ref.dtype)

    @pl.when((j >= n_qa) & (j < n_plain_end))
    def _():
        o_ref[...] = acc.astype(o_ref.dtype)

    @pl.when((j >= n_plain_end) & (j < n_qb_end))
    def _():
        for h in range(heads_per_tile):
            sl = slice(h * HEAD_DIM, (h + 1) * HEAD_DIM)
            o_ref[:, sl] = rope(acc[:, sl], gq_ref[...], scale * LOG2E).astype(o_ref.dtype)

    @pl.when(j >= n_qb_end)
    def _():
        for h in range(heads_per_tile):
            sl = slice(h * HEAD_DIM, (h + 1) * HEAD_DIM)
            if h < N_KV_GQA:
                o_ref[:, sl] = rope(acc[:, sl], gk_ref[...], 1.0).astype(o_ref.dtype)
            else:
                o_ref[:, sl] = acc[:, sl].astype(o_ref.dtype)


def _rope_tables(L):
    t = jnp.arange(L)
    row = (t // GRID_W).astype(F32)
    col = (t % GRID_W).astype(F32)
    inv = ROPE_THETA ** (-jnp.arange(0, AXIS_DIM, 2, dtype=F32) / AXIS_DIM)
    ang_r = row[:, None] * inv
    ang_c = col[:, None] * inv
    ang = jnp.concatenate([ang_r, ang_r, ang_c, ang_c], axis=-1)
    cos, sin = jnp.cos(ang), jnp.sin(ang)
    first_half = (np.arange(HEAD_DIM) % AXIS_DIM) < AXIS_DIM // 2
    sa = jnp.where(first_half, -sin, 0.0)
    sb = jnp.where(first_half, 0.0, sin)
    return cos, sa, sb


def _in_proj(x2, g1, w_bf, tables, gq, gk, L):
    T, D = x2.shape
    tm = _tile(L, 1024)
    n_pos = L // tm
    cos, sa, sb = tables
    assert IN_COLS % PROJ_TN == 0 and (IN_COLS - 2 * KV_WIDTH_GQA) % PROJ_TN == 0
    pos_spec = pl.BlockSpec((tm, HEAD_DIM), lambda i, j: (i % n_pos, 0))
    vec = lambda n: pl.BlockSpec((1, n), lambda i, j: (0, 0))
    return pl.pallas_call(
        _in_proj_body,
        grid=(T // tm, IN_COLS // PROJ_TN),
        in_specs=[
            pl.BlockSpec((tm, D), lambda i, j: (i, 0)),
            vec(D),
            pl.BlockSpec((D, PROJ_TN), lambda i, j: (0, j)),
            pos_spec, pos_spec, pos_spec,
            vec(HEAD_DIM), vec(HEAD_DIM),
        ],
        out_specs=pl.BlockSpec((tm, PROJ_TN), lambda i, j: (i, j)),
        out_shape=jax.ShapeDtypeStruct((T, IN_COLS), BF16),
        scratch_shapes=[pltpu.VMEM((tm, D), BF16)],
        compiler_params=pltpu.CompilerParams(
            dimension_semantics=("arbitrary", "arbitrary"), vmem_limit_bytes=VMEM_LIMIT),
        name="in_proj",
    )(x2, g1.reshape(1, D), w_bf, cos, sa, sb, gq.reshape(1, HEAD_DIM), gk.reshape(1, HEAD_DIM))


def _na_bias_tables(rpb):
    a = np.arange(NA_QROWS)[:, None, None, None, None]
    c = np.arange(GRID_W)[None, :, None, None, None]
    s = np.arange(NA_SEG)[None, None, :, None, None]
    j = np.arange(NA_QROWS)[None, None, None, :, None]
    kc = np.arange(GRID_W)[None, None, None, None, :]
    rel = NA_QROWS * (s - 1) + j
    dr = rel - a
    half = NA_KH // 2
    row_ok = np.stack([
        (rel >= 0) & (rel < NA_KH) & (a >= 0),
        (dr >= -half) & (dr < NA_KH - half),
        (rel >= NA_QROWS - NA_KH) & (rel < NA_QROWS) & (a >= 0),
    ])
    ws = np.clip(c - NA_KW // 2, 0, GRID_W - NA_KW)
    col_ok = (kc >= ws) & (kc < ws + NA_KW)
    ok = np.broadcast_to(row_ok & col_ok, (3, NA_QROWS, GRID_W, NA_SEG, NA_QROWS, GRID_W))
    ok = ok.reshape(3, NA_QB, NA_SEG * NA_QB)
    ridx = np.clip(dr + NA_KH - 1, 0, 2 * NA_KH - 2).reshape(NA_QROWS, NA_SEG * NA_QROWS)
    rsel = rpb.astype(F32)[:, ridx, :]
    cidx = np.clip(kc - c + NA_KW - 1, 0, 2 * NA_KW - 2).reshape(GRID_W, GRID_W)
    onehot = (np.arange(2 * NA_KW - 1)[:, None, None] == cidx[None]).astype(np.float32)
    vals = jnp.einsum("haxi,ick->hacxk", rsel, onehot, precision=lax.Precision.HIGHEST)
    vals = vals.reshape(rpb.shape[0], NA_QB, NA_SEG * NA_QB) * LOG2E
    return jnp.where(ok[:, None], vals[None], NEG)


def _na_body(q_ref, k0_ref, k1_ref, k2_ref, v0_ref, v1_ref, v2_ref, b_ref, o_ref):
    dn = (((1,), (1,)), ((), ()))
    ones = jnp.ones((NA_QB, HEAD_DIM), BF16)
    for h in range(N_HEADS_NA):
        sl = slice(h * HEAD_DIM, (h + 1) * HEAD_DIM)
        q = q_ref[:, sl]
        s = jnp.concatenate(
            [lax.dot_general(q, k[:, sl], dn, preferred_element_type=F32) for k in (k0_ref, k1_ref, k2_ref)],
            axis=1)
        s = s + b_ref[0, h]
        p = jnp.exp2(s - jnp.max(s, axis=-1, keepdims=True)).astype(BF16)
        o = jnp.zeros((NA_QB, 2 * HEAD_DIM), F32)
        for seg, v in enumerate((v0_ref, v1_ref, v2_ref)):
            v1 = jnp.concatenate([v[:, sl], ones], axis=1)
            o = o + jnp.dot(p[:, seg * NA_QB:(seg + 1) * NA_QB], v1, preferred_element_type=F32)
        o_ref[:, sl] = (o[:, :HEAD_DIM] / o[:, HEAD_DIM:]).astype(o_ref.dtype)


def _na_attention(proj, bias, B, L):
    T = proj.shape[0]
    rows = L // GRID_W
    assert L % GRID_W == 0 and rows % NA_QROWS == 0 and rows >= NA_KH
    nblk = rows // NA_QROWS
    assert COL_QA % N_HEADS_NA == 0 and COL_KA % N_HEADS_NA == 0 and COL_VA % N_HEADS_NA == 0

    def blk_spec(col0, shift):
        def imap(b, i):
            return (b * nblk + jnp.clip(i + shift, 0, nblk - 1), col0 // N_HEADS_NA)
        return pl.BlockSpec((NA_QB, WIDTH_NA), imap)

    def variant(b, i):
        return (jnp.where(i == 0, 0, jnp.where(i == nblk - 1, 2, 1)), 0, 0, 0)

    return pl.pallas_call(
        _na_body,
        grid=(B, nblk),
        in_specs=[
            blk_spec(COL_QA, 0),
            blk_spec(COL_KA, -1), blk_spec(COL_KA, 0), blk_spec(COL_KA, 1),
            blk_spec(COL_VA, -1), blk_spec(COL_VA, 0), blk_spec(COL_VA, 1),
            pl.BlockSpec((1, N_HEADS_NA, NA_QB, NA_SEG * NA_QB), variant),
        ],
        out_specs=pl.BlockSpec((NA_QB, WIDTH_NA), lambda b, i: (b * nblk + i, 0)),
        out_shape=jax.ShapeDtypeStruct((T, WIDTH_NA), BF16),
        compiler_params=pltpu.CompilerParams(
            dimension_semantics=("arbitrary", "arbitrary"), vmem_limit_bytes=VMEM_LIMIT),
        name="na_attn",
    )(proj, proj, proj, proj, proj, proj, proj, bias)


def _gqa_body(q_ref, k_ref, v_ref, o_ref, v1_scr, q_scr, s0_scr, s1_scr, p0_scr, p1_scr, al0_scr, al1_scr,
              m_scr, acc_scr, *, tk, nk):
    tq = q_ref.shape[0]
    L = k_ref.shape[0]
    dn = (((1,), (1,)), ((), ()))
    set0 = (s0_scr, p0_scr, al0_scr)
    set1 = (s1_scr, p1_scr, al1_scr)

    @pl.when(pl.program_id(2) == 0)
    def _():
        v1_scr[:, :HEAD_DIM] = v_ref[...]
        v1_scr[:, HEAD_DIM:] = jnp.ones((L, HEAD_DIM), BF16)

    for g in range(GQA_GROUP):
        q_scr[g * tq:(g + 1) * tq, :] = q_ref[:, g * HEAD_DIM:(g + 1) * HEAD_DIM]
    m_scr[...] = jnp.full(m_scr.shape, -jnp.inf, F32)
    acc_scr[...] = jnp.zeros(acc_scr.shape, F32)

    def scores(c, bufs):
        s_scr, _, _ = bufs
        off = pl.multiple_of(c * tk, tk)
        s_scr[...] = lax.dot_general(q_scr[...], k_ref[pl.ds(off, tk), :], dn, preferred_element_type=F32)

    def softmax(bufs):
        s_scr, p_scr, al_scr = bufs
        s = s_scr[...]
        m_prev = m_scr[...]
        m_new = jnp.maximum(m_prev, jnp.max(s, axis=-1, keepdims=True))
        al_scr[...] = jnp.exp2(m_prev - m_new)
        p_scr[...] = jnp.exp2(s - jnp.concatenate([m_new] * (tk // HEAD_DIM), axis=1)).astype(BF16)
        m_scr[...] = m_new

    def weighted_values(c, bufs):
        _, p_scr, al_scr = bufs
        off = pl.multiple_of(c * tk, tk)
        pv = jnp.dot(p_scr[...], v1_scr[pl.ds(off, tk), :], preferred_element_type=F32)
        al = al_scr[...]
        acc_scr[...] = jnp.concatenate([al, al], axis=1) * acc_scr[...] + pv

    scores(0, set0)
    scores(1, set1)
    softmax(set0)

    def pair(i2, carry):
        i = 2 + 2 * i2
        scores(i, set0)
        softmax(set1)
        weighted_values(i - 2, set0)
        scores(i + 1, set1)
        softmax(set0)
        weighted_values(i - 1, set1)
        return carry

    lax.fori_loop(0, (nk - 2) // 2, pair, 0)
    softmax(set1)
    weighted_values(nk - 2, set0)
    weighted_values(nk - 1, set1)
    o = acc_scr[:, :HEAD_DIM] / acc_scr[:, HEAD_DIM:]
    for g in range(GQA_GROUP):
        o_ref[:, g * HEAD_DIM:(g + 1) * HEAD_DIM] = o[g * tq:(g + 1) * tq].astype(o_ref.dtype)


def _gqa_attention(proj, B, L):
    T = proj.shape[0]
    tq = _tile(L, 256)
    tk = _tile(L, 1024 if L >= 8192 else 512)
    nk = L // tk
    assert tk % HEAD_DIM == 0 and nk >= 2 and nk % 2 == 0
    nq = L // tq
    gw = GQA_GROUP * HEAD_DIM
    rows = GQA_GROUP * tq
    vm = pltpu.VMEM
    return pl.pallas_call(
        functools.partial(_gqa_body, tk=tk, nk=nk),
        grid=(B, N_KV_GQA, nq),
        in_specs=[
            pl.BlockSpec((tq, gw), lambda b, kv, qi: (b * nq + qi, COL_QB // GQA_GROUP + kv)),
            pl.BlockSpec((L, HEAD_DIM), lambda b, kv, qi: (b, COL_KB + kv)),
            pl.BlockSpec((L, HEAD_DIM), lambda b, kv, qi: (b, COL_VB + kv)),
        ],
        out_specs=pl.BlockSpec((tq, gw), lambda b, kv, qi: (b * nq + qi, kv)),
        out_shape=jax.ShapeDtypeStruct((T, WIDTH_GQA), BF16),
        scratch_shapes=[
            vm((L, 2 * HEAD_DIM), BF16), vm((rows, HEAD_DIM), BF16),
            vm((rows, tk), F32), vm((rows, tk), F32), vm((rows, tk), BF16), vm((rows, tk), BF16),
            vm((rows, HEAD_DIM), F32), vm((rows, HEAD_DIM), F32),
            vm((rows, HEAD_DIM), F32), vm((rows, 2 * HEAD_DIM), F32)],
        compiler_params=pltpu.CompilerParams(
            dimension_semantics=("arbitrary", "arbitrary", "arbitrary"), vmem_limit_bytes=VMEM_LIMIT),
        name="gqa_attn",
    )(proj, proj, proj)


def _out_proj_body(oa_ref, ob_ref, x_ref, ga_ref, gb_ref, w_ref, g2_ref, wr_ref, br_ref,
                   x1_ref, eid_ref, gate_ref, rank_ref, cnt_ref, cnt_scr):
    @pl.when(pl.program_id(0) == 0)
    def _():
        cnt_scr[...] = jnp.zeros(cnt_scr.shape, F32)

    mix = jnp.concatenate(
        [_rms(oa_ref[...].astype(F32), ga_ref[...]).astype(BF16),
         _rms(ob_ref[...].astype(F32), gb_ref[...]).astype(BF16)], axis=1)
    x1 = x_ref[...] + jnp.dot(mix, w_ref[...], preferred_element_type=F32)
    x1_ref[...] = x1

    h2 = _rms(x1, g2_ref[...])
    h_hi = h2.astype(BF16)
    h_lo = (h2 - h_hi.astype(F32)).astype(BF16)
    a = (jnp.dot(h_hi, wr_ref[...], preferred_element_type=F32)
         + jnp.dot(h_lo, wr_ref[...], preferred_element_type=F32) + br_ref[...])
    at = a.T
    lt = at[:ROUTER_HALF] + at[ROUTER_HALF:]

    lg = [lt[g:g + 1] for g in range(N_GROUPS)]
    gmax = functools.reduce(jnp.maximum, lg)
    grp = jnp.full_like(gmax, float(N_GROUPS))
    for g in reversed(range(N_GROUPS)):
        grp = jnp.where(lg[g] == gmax, float(g), grp)
    denom = functools.reduce(lambda u, v: u + v, [jnp.exp(x - gmax) for x in lg])
    p_grp = 1.0 / denom

    le = jnp.zeros((EXPERTS_PER_GROUP, lt.shape[1]), F32)
    for g in range(N_GROUPS):
        r0 = ROUTER_EXPERT_ROW0 + g * EXPERTS_PER_GROUP
        le = jnp.where(grp == g, lt[r0:r0 + EXPERTS_PER_GROUP], le)
    idx = lax.broadcasted_iota(jnp.int32, le.shape, 0).astype(F32)
    none = float(EXPERTS_PER_GROUP)
    v1 = jnp.max(le, axis=0, keepdims=True)
    i1 = jnp.min(jnp.where(le == v1, idx, none), axis=0, keepdims=True)
    rest = jnp.where(idx == i1, -jnp.inf, le)
    v2 = jnp.max(rest, axis=0, keepdims=True)
    i2 = jnp.min(jnp.where(rest == v2, idx, none), axis=0, keepdims=True)
    e2 = jnp.exp(v2 - v1)
    den = 1.0 + e2
    base = grp * float(EXPERTS_PER_GROUP)
    e_first, e_second = base + i1, base + i2
    eid_ref[0:1, :] = e_first.astype(jnp.int32)
    eid_ref[1:2, :] = e_second.astype(jnp.int32)
    gate_ref[0:1, :] = p_grp * (1.0 / den)
    gate_ref[1:2, :] = p_grp * (e2 / den)

    tm = lt.shape[1]
    eio = lax.broadcasted_iota(jnp.int32, (N_EXPERTS, tm), 0).astype(F32)
    hit1 = (eio == e_first).astype(F32)
    hit2 = (eio == e_second).astype(F32)
    hits = hit1 + hit2
    before = (lax.broadcasted_iota(jnp.int32, (tm, tm), 0)
              < lax.broadcasted_iota(jnp.int32, (tm, tm), 1)).astype(BF16)
    prefix = jnp.dot(hits.astype(BF16), before, preferred_element_type=F32)
    run = cnt_scr[...]
    tot = prefix + jnp.concatenate([run] * (tm // run.shape[1]), axis=1)
    rank_ref[0:1, :] = jnp.sum(hit1 * tot, axis=0, keepdims=True).astype(jnp.int32)
    rank_ref[1:2, :] = jnp.sum(hit2 * tot, axis=0, keepdims=True).astype(jnp.int32)
    run = run + jnp.sum(hits, axis=1, keepdims=True)
    cnt_scr[...] = run
    cnt_ref[...] = run.astype(jnp.int32)


def _router_weights(w_rg, b_rg, w_re, b_re):
    D = w_rg.shape[0]
    w = jnp.zeros((D, ROUTER_HALF), F32)
    w = w.at[:, :N_GROUPS].set(w_rg.astype(F32))
    w_e = jnp.transpose(w_re.astype(F32), (1, 0, 2)).reshape(D, N_EXPERTS)
    w = w.at[:, ROUTER_EXPERT_ROW0:ROUTER_EXPERT_ROW0 + N_EXPERTS].set(w_e)
    hi = w.astype(BF16)
    lo = (w - hi.astype(F32)).astype(BF16)
    b = jnp.zeros((1, ROUTER_LANES), F32)
    b = b.at[0, :N_GROUPS].set(b_rg.astype(F32))
    b = b.at[0, ROUTER_EXPERT_ROW0:ROUTER_EXPERT_ROW0 + N_EXPERTS].set(b_re.astype(F32).reshape(-1))
    return jnp.concatenate([hi, lo], axis=1), b


def _out_proj(o_a, o_b, x2, g_a, g_b, w_bf, g2, wr, br):
    T, D = x2.shape
    tm = _tile(T, 256)
    row = lambda n: pl.BlockSpec((tm, n), lambda i: (i, 0))
    vec = lambda n: pl.BlockSpec((1, n), lambda i: (0, 0))
    full = lambda a: pl.BlockSpec(a.shape, lambda i: (0, 0))
    per_tok = pl.BlockSpec((TOP_K, tm), lambda i: (0, i))
    return pl.pallas_call(
        _out_proj_body,
        grid=(T // tm,),
        in_specs=[row(WIDTH_NA), row(WIDTH_GQA), row(D), vec(WIDTH_NA), vec(WIDTH_GQA), full(w_bf),
                  vec(D), full(wr), vec(ROUTER_LANES)],
        out_specs=[row(D), per_tok, per_tok, per_tok, pl.BlockSpec((N_EXPERTS, HEAD_DIM), lambda i: (0, 0))],
        out_shape=[jax.ShapeDtypeStruct((T, D), F32), jax.ShapeDtypeStruct((TOP_K, T), jnp.int32),
                   jax.ShapeDtypeStruct((TOP_K, T), F32), jax.ShapeDtypeStruct((TOP_K, T), jnp.int32),
                   jax.ShapeDtypeStruct((N_EXPERTS, HEAD_DIM), jnp.int32)],
        scratch_shapes=[pltpu.VMEM((N_EXPERTS, HEAD_DIM), F32)],
        compiler_params=pltpu.CompilerParams(dimension_semantics=("arbitrary",), vmem_limit_bytes=VMEM_LIMIT),
        name="out_proj",
    )(o_a, o_b, x2, g_a.reshape(1, -1), g_b.reshape(1, -1), w_bf, g2.reshape(1, D), wr, br)


def _dispatch_body(zstart_ref, nused_ref, d_ref, x1_ref, g2_ref, xs_hbm, hbuf, zbuf, sem, zsem, *,
                   tm, blk, n_tiles, n_blocks):
    i = pl.program_id(0)
    slot = i % 2

    def zero_copy(row0):
        return pltpu.make_async_copy(zbuf, xs_hbm.at[pl.ds(pl.multiple_of(row0, blk), blk)], zsem)

    @pl.when(i == 0)
    def _():
        zbuf[...] = jnp.zeros(zbuf.shape, F32)

        def each(action):
            def per_expert(e, carry):
                @pl.when(zstart_ref[e] >= 0)
                def _():
                    action(zero_copy(jnp.maximum(zstart_ref[e], 0)))
                return carry
            lax.fori_loop(0, N_EXPERTS, per_expert, 0)

            def per_tail_block(b, carry):
                action(zero_copy(b * blk))
                return carry
            lax.fori_loop(nused_ref[0], n_blocks, per_tail_block, 0)

        each(lambda copy: copy.start())
        each(lambda copy: copy.wait())

    def wait_rows(s):
        for _ in range(TOP_K):
            pltpu.make_async_copy(hbuf.at[s], xs_hbm.at[pl.ds(0, tm)], sem.at[s]).wait()

    @pl.when(i >= 2)
    def _():
        wait_rows(slot)

    hbuf[slot] = _rms(x1_ref[...], g2_ref[...])

    def issue(g, carry):
        for u in range(GATHER_UNROLL):
            t = g * GATHER_UNROLL + u
            for k in range(TOP_K):
                pltpu.make_async_copy(hbuf.at[slot, pl.ds(t, 1)], xs_hbm.at[pl.ds(d_ref[0, 0, k * tm + t], 1)],
                                      sem.at[slot]).start(priority=k)
        return carry
    lax.fori_loop(0, tm // GATHER_UNROLL, issue, 0)

    @pl.when(i == n_tiles - 1)
    def _():
        wait_rows(slot)
        if n_tiles > 1:
            wait_rows(1 - slot)


def _dispatch(x1, g2, dest_km, zstart, nused, n_rows, tm, blk):
    T, D = x1.shape
    n_tiles = T // tm
    assert tm % GATHER_UNROLL == 0
    grid_spec = pltpu.PrefetchScalarGridSpec(
        num_scalar_prefetch=2,
        grid=(n_tiles,),
        in_specs=[
            pl.BlockSpec((1, 1, TOP_K * tm), lambda i, z, nu: (i, 0, 0), memory_space=pltpu.SMEM),
            pl.BlockSpec((tm, D), lambda i, z, nu: (i, 0)),
            pl.BlockSpec((1, D), lambda i, z, nu: (0, 0)),
        ],
        out_specs=pl.BlockSpec(memory_space=pl.ANY),
        scratch_shapes=[pltpu.VMEM((2, tm, D), F32), pltpu.VMEM((blk, D), F32),
                        pltpu.SemaphoreType.DMA((2,)), pltpu.SemaphoreType.DMA(())],
    )
    return pl.pallas_call(
        functools.partial(_dispatch_body, tm=tm, blk=blk, n_tiles=n_tiles, n_blocks=n_rows // blk),
        grid_spec=grid_spec,
        out_shape=jax.ShapeDtypeStruct((n_rows, D), F32),
        compiler_params=pltpu.CompilerParams(dimension_semantics=("arbitrary",), vmem_limit_bytes=VMEM_LIMIT),
        name="dispatch",
    )(zstart, nused, dest_km, x1, g2.reshape(1, D))


def _experts_body(bexp_ref, nused_ref, xs_ref, wg_ref, wu_ref, wd_ref, o_ref):
    i = pl.program_id(0)

    @pl.when(i < nused_ref[0])
    def _():
        h = xs_ref[...].astype(BF16)
        g = jnp.dot(h, wg_ref[0], preferred_element_type=F32)
        u = jnp.dot(h, wu_ref[0], preferred_element_type=F32)
        act = (g * (1.0 / (1.0 + jnp.exp(-g))) * u).astype(BF16)
        o_ref[...] = jnp.dot(act, wd_ref[0], preferred_element_type=F32)

    @pl.when(i >= nused_ref[0])
    def _():
        o_ref[...] = jnp.zeros_like(o_ref)


def _experts(xs, wg, wu, wd, block_expert, nused, blk):
    n_rows, D = xs.shape
    n_blocks = n_rows // blk
    wspec = lambda a: pl.BlockSpec((1,) + a.shape[1:], lambda i, be, nu: (be[i], 0, 0))
    grid_spec = pltpu.PrefetchScalarGridSpec(
        num_scalar_prefetch=2,
        grid=(n_blocks,),
        in_specs=[
            pl.BlockSpec((blk, D), lambda i, be, nu: (jnp.minimum(i, nu[0] - 1), 0)),
            wspec(wg), wspec(wu), wspec(wd),
        ],
        out_specs=pl.BlockSpec((blk, D), lambda i, be, nu: (i, 0)),
    )
    return pl.pallas_call(
        _experts_body,
        grid_spec=grid_spec,
        out_shape=jax.ShapeDtypeStruct((n_rows, D), F32),
        compiler_params=pltpu.CompilerParams(dimension_semantics=("arbitrary",), vmem_limit_bytes=VMEM_LIMIT),
        name="experts",
    )(block_expert, nused, xs, wg, wu, wd)


def _row_gather(src_hbm, idx_ref, dst, sem, n):
    assert n % GATHER_UNROLL == 0

    def issue(g, carry):
        for u in range(GATHER_UNROLL):
            r = g * GATHER_UNROLL + u
            pltpu.make_async_copy(src_hbm.at[pl.ds(idx_ref[0, 0, r], 1)], dst.at[pl.ds(r, 1)], sem).start(
                priority=u % 2)
        return carry
    lax.fori_loop(0, n // GATHER_UNROLL, issue, 0)


def _row_gather_wait(src_hbm, dst, sem, n):
    assert dst.shape[0] == n
    pltpu.make_async_copy(src_hbm.at[pl.ds(0, n)], dst, sem).wait()


def _combine_body(d_ref, d_next_ref, x1_ref, gate_ref, yb_hbm, gf_ref, o_ref, ybuf, sem, *, tm, n_tiles):
    i = pl.program_id(0)
    slot = i % 2

    @pl.when(i == 0)
    def _():
        _row_gather(yb_hbm, d_ref, ybuf.at[0], sem.at[0], TOP_K * tm)

    @pl.when(i + 1 < n_tiles)
    def _():
        _row_gather(yb_hbm, d_next_ref, ybuf.at[1 - slot], sem.at[1 - slot], TOP_K * tm)

    _row_gather_wait(yb_hbm, ybuf.at[slot], sem.at[slot], TOP_K * tm)
    gt = gate_ref[...]
    y = x1_ref[...] + (ybuf[slot, :tm] * gt[:, 0:1] + ybuf[slot, tm:] * gt[:, 1:2])
    o_ref[...] = _rms(y, gf_ref[...])


def _combine(x1, yb, dest_km, gate_tk, g_f, tm):
    T, D = x1.shape
    n_tiles = T // tm
    smem_blk = lambda shift: pl.BlockSpec(
        (1, 1, TOP_K * tm), lambda i: (jnp.minimum(i + shift, n_tiles - 1), 0, 0), memory_space=pltpu.SMEM)
    return pl.pallas_call(
        functools.partial(_combine_body, tm=tm, n_tiles=n_tiles),
        grid=(n_tiles,),
        in_specs=[
            smem_blk(0), smem_blk(1),
            pl.BlockSpec((tm, D), lambda i: (i, 0)),
            pl.BlockSpec((tm, TOP_K), lambda i: (i, 0)),
            pl.BlockSpec(memory_space=pl.ANY),
            pl.BlockSpec((1, D), lambda i: (0, 0)),
        ],
        out_specs=pl.BlockSpec((tm, D), lambda i: (i, 0)),
        out_shape=jax.ShapeDtypeStruct((T, D), F32),
        scratch_shapes=[pltpu.VMEM((2, TOP_K * tm, D), F32), pltpu.SemaphoreType.DMA((2,))],
        compiler_params=pltpu.CompilerParams(dimension_semantics=("arbitrary",), vmem_limit_bytes=VMEM_LIMIT),
        name="combine",
    )(dest_km, dest_km, x1, gate_tk, yb, g_f.reshape(1, D))


def _dispatch_plan(eid, rank, cnt, blk):
    T = eid.shape[1]
    counts = cnt[:, 0]
    padded = (counts + blk - 1) // blk * blk
    pad_end = jnp.cumsum(padded)
    pad_start = pad_end - padded
    dest = (pad_start[eid] + rank).astype(jnp.int32)
    n_blocks = T * TOP_K // blk + N_EXPERTS
    block_expert = jnp.minimum(
        jnp.searchsorted(pad_end, jnp.arange(n_blocks) * blk, side="right"), N_EXPERTS - 1).astype(jnp.int32)
    nused = (pad_end[-1] // blk).astype(jnp.int32).reshape(1)
    zstart = jnp.where(counts > 0, pad_end - blk, -1).astype(jnp.int32)
    return dest, block_expert, nused, zstart, n_blocks * blk


def _trunk(x, p, moe_blk=256):
    B, L, D = x.shape
    T = B * L
    x2 = x.reshape(T, D)
    proj = _in_proj(x2, p["g1"], p["w_in"], _rope_tables(L), p["g_q"], p["g_k"], L)
    o_a = _na_attention(proj, p["na_bias"], B, L)
    o_b = _gqa_attention(proj, B, L)
    x1, eid, gate, rank, cnt = _out_proj(
        o_a, o_b, x2, p["g_out_a"], p["g_out_b"], p["w_out"], p["g2"], p["wr"], p["br"])
    dest, block_expert, nused, zstart, n_rows = _dispatch_plan(eid, rank, cnt, moe_blk)
    tm = _tile(T, 256)
    dest_km = dest.reshape(TOP_K, T // tm, tm).transpose(1, 0, 2).reshape(T // tm, 1, TOP_K * tm)
    xs = _dispatch(x1, p["g2"], dest_km, zstart, nused, n_rows, tm, moe_blk)
    yb = _experts(xs, p["w_gate"], p["w_up"], p["w_down"], block_expert, nused, moe_blk)
    y = _combine(x1, yb, dest_km, gate.T, p["g_f"], tm)
    return y.reshape(B, L, D)


def _prepare(g_norm1, w_in, rpb, g_q, g_k, g_out_a, g_out_b, w_out, g_norm2, w_router_group, b_router_group,
             w_router_expert, b_router_expert, w_gate, w_up, w_down, g_norm_f):
    wr, br = _router_weights(w_router_group, b_router_group, w_router_expert, b_router_expert)
    return dict(
        g1=g_norm1, w_in=w_in.astype(BF16), na_bias=_na_bias_tables(rpb), g_q=g_q, g_k=g_k,
        g_out_a=g_out_a, g_out_b=g_out_b, w_out=w_out.astype(BF16), g2=g_norm2, wr=wr, br=br,
        w_gate=w_gate.astype(BF16), w_up=w_up.astype(BF16), w_down=w_down.astype(BF16), g_f=g_norm_f)


def kernel(x_prompt, x_sample, g_norm1, w_in, rpb, g_q, g_k, g_out_a, g_out_b, w_out, g_norm2, w_router_group,
           b_router_group, w_router_expert, b_router_expert, w_gate, w_up, w_down, g_norm_f):
    assert g_norm1.shape[0] == 1, "single-layer trunk"
    p = _prepare(g_norm1[0], w_in[0], rpb[0], g_q[0], g_k[0], g_out_a[0], g_out_b[0], w_out[0], g_norm2[0],
                 w_router_group[0], b_router_group[0], w_router_expert[0], b_router_expert[0],
                 w_gate[0], w_up[0], w_down[0], g_norm_f)
    return (_trunk(x_prompt, p), _trunk(x_sample, p))
```
